```python
import math
import jax, jax.numpy as jnp
from jax import lax
import numpy as np

D_MODEL = 1024
BATCH = 8
SEQ = 4096
DEPTH = 2

N_BRANCH = 4
BRANCH_W = D_MODEL // 4
HEAD_DIM = 64
SB_HEADS = BRANCH_W // HEAD_DIM
SB_BLOCK = 128
SG_CHUNK = 128
SG_GROUPS = 4
SG_GD = BRANCH_W // SG_GROUPS
POOL_WINDOWS = (2, 4, 8, 16)
POOL_GD = BRANCH_W // len(POOL_WINDOWS)
CONV_W = 31
D_FF = 2816
LN_EPS = 1e-5
DN_ALPHA = (2.0 * DEPTH) ** 0.25
DN_BETA = (8.0 * DEPTH) ** -0.25

A_Q0, A_K0, A_V0 = 0, BRANCH_W, 2 * BRANCH_W
B0 = 3 * BRANCH_W
C0 = B0 + 2 * BRANCH_W
D0 = C0 + BRANCH_W
IN_COLS = D0 + 2 * BRANCH_W

kernel_name = "hybrid_gated_sb_gmlp_pool_conv_block"


def layer_norm(x, g, b):
    xf = x.astype(jnp.float32)
    mu = jnp.mean(xf, axis=-1, keepdims=True)
    var = jnp.mean(jnp.square(xf - mu), axis=-1, keepdims=True)
    y = (xf - mu) * lax.rsqrt(var + LN_EPS) * g.astype(jnp.float32) + b.astype(jnp.float32)
    return y.astype(x.dtype)


def swiglu_ffn(x, w_in, w_out):
    a, u = jnp.split(x @ w_in, 2, axis=-1)
    return (jax.nn.silu(a) * u) @ w_out


def stick_breaking_attention(q, k, v):
    S = q.shape[1]
    scale = HEAD_DIM ** -0.5
    outs = []
    for i in range(S // SB_BLOCK):
        q0 = i * SB_BLOCK
        kend = q0 + SB_BLOCK
        qb = q[:, q0:kend]
        kb = k[:, :kend]
        vb = v[:, :kend]
        z = jnp.einsum('bqhd,bkhd->bhqk', qb, kb).astype(jnp.float32) * scale
        t_pos = q0 + jnp.arange(SB_BLOCK)[:, None]
        s_pos = jnp.arange(kend)[None, :]
        causal = s_pos < t_pos
        log_keep = jnp.where(causal, jax.nn.log_sigmoid(-z), 0.0)
        after = lax.cumsum(log_keep, axis=3, reverse=True) - log_keep
        w = jnp.where(causal, jnp.exp(jax.nn.log_sigmoid(z) + after), 0.0)
        outs.append(jnp.einsum('bhqk,bkhd->bqhd', w.astype(vb.dtype), vb))
    return jnp.concatenate(outs, axis=1)


def chunked_spatial_gating(uv, ln_g, ln_b, w_s, b_s):
    B, S, _ = uv.shape
    u, v = jnp.split(uv, 2, axis=-1)
    v = layer_norm(v, ln_g, ln_b)
    vc = v.reshape(B, S // SG_CHUNK, SG_CHUNK, SG_GROUPS, SG_GD)
    mask = jnp.tril(jnp.ones((SG_CHUNK, SG_CHUNK), dtype=bool))
    ws = jnp.where(mask, w_s, jnp.zeros_like(w_s))
    mixed = jnp.einsum('gts,bcsgd->bctgd', ws, vc) + jnp.transpose(b_s)[None, None, :, :, None]
    return u * mixed.reshape(B, S, BRANCH_W)


def multiscale_pool(p, w_grp, scale):
    B, S, _ = p.shape
    pf = p.astype(jnp.float32).reshape(B, S, len(POOL_WINDOWS), POOL_GD)
    cs = jnp.cumsum(pf, axis=1)
    pos = jnp.arange(S, dtype=jnp.float32)
    outs = []
    for gi, win in enumerate(POOL_WINDOWS):
        c = cs[:, :, gi]
        prev = jnp.pad(c, ((0, 0), (win, 0), (0, 0)))[:, :S]
        cnt = jnp.minimum(pos + 1.0, float(win))[None, :, None]
        outs.append((c - prev) / cnt - pf[:, :, gi])
    pooled = jnp.stack(outs, axis=2).astype(p.dtype)
    y = jnp.einsum('bsgc,gcd->bsgd', pooled, w_grp).reshape(B, S, BRANCH_W)
    return y * scale


def conformer_conv(h, w_dw, b_dw, ln_g, ln_b):
    a, g = jnp.split(h, 2, axis=-1)
    y = a * jax.nn.sigmoid(g)
    y = lax.conv_general_dilated(
        y, w_dw[:, None, :], window_strides=(1,), padding=[(CONV_W - 1, 0)],
        dimension_numbers=('NWC', 'WIO', 'NWC'), feature_group_count=BRANCH_W) + b_dw
    y = layer_norm(y, ln_g, ln_b)
    return jax.nn.silu(y)


def hybrid_mixer(x, w_in, gate_w, gate_b, branch_w, out_w, sg_ln_g, sg_ln_b, sg_w, sg_b,
                 pool_w, pool_scale, conv_w, conv_b, conv_ln_g, conv_ln_b):
    B, S, _ = x.shape
    h = x @ w_in
    q = h[..., A_Q0:A_K0].reshape(B, S, SB_HEADS, HEAD_DIM)
    k = h[..., A_K0:A_V0].reshape(B, S, SB_HEADS, HEAD_DIM)
    v = h[..., A_V0:B0].reshape(B, S, SB_HEADS, HEAD_DIM)
    y_a = stick_breaking_attention(q, k, v).reshape(B, S, BRANCH_W)
    y_b = chunked_spatial_gating(jax.nn.gelu(h[..., B0:C0]), sg_ln_g, sg_ln_b, sg_w, sg_b)
    y_c = multiscale_pool(h[..., C0:D0], pool_w, pool_scale)
    y_d = conformer_conv(h[..., D0:IN_COLS], conv_w, conv_b, conv_ln_g, conv_ln_b)
    merged = jnp.zeros_like(x)
    for n, y in enumerate((y_a, y_b, y_c, y_d)):
        gate = jax.nn.sigmoid(x @ gate_w[n] + gate_b[n])
        merged = merged + gate * (y @ branch_w[n])
    return merged @ out_w


def setup_inputs(seed: int = 0) -> dict:
    key = jax.random.key(seed)
    ks = jax.random.split(key, 24)
    f32 = jnp.float32
    L, D, BW = DEPTH, D_MODEL, BRANCH_W
    nrm = lambda k, shape, s: jax.random.normal(k, shape, f32) * s
    return {
        "x": jax.random.normal(ks[0], (BATCH, SEQ, D), f32),
        "ln_g": 1.0 + nrm(ks[1], (L, 3, D), 0.02),
        "ln_b": nrm(ks[2], (L, 3, D), 0.02),
        "ffn_w_in": nrm(ks[3], (L, 2, D, 2 * D_FF), D ** -0.5),
        "ffn_w_out": nrm(ks[4], (L, 2, D_FF, D), DN_BETA * D_FF ** -0.5),
        "mix_w_in": nrm(ks[5], (L, D, IN_COLS), D ** -0.5),
        "gate_w": nrm(ks[6], (L, N_BRANCH, D, D), D ** -0.5),
        "gate_b": nrm(ks[7], (L, N_BRANCH, D), 0.02),
        "branch_w": nrm(ks[8], (L, N_BRANCH, BW, D), BW ** -0.5),
        "out_w": nrm(ks[9], (L, D, D), DN_BETA * D ** -0.5),
        "sg_ln_g": 1.0 + nrm(ks[10], (L, BW), 0.02),
        "sg_ln_b": nrm(ks[11], (L, BW), 0.02),
        "sg_w": nrm(ks[12], (L, SG_GROUPS, SG_CHUNK, SG_CHUNK), SG_CHUNK ** -0.5),
        "sg_b": 1.0 + nrm(ks[13], (L, SG_GROUPS, SG_CHUNK), 0.02),
        "pool_w": nrm(ks[14], (L, len(POOL_WINDOWS), POOL_GD, POOL_GD), POOL_GD ** -0.5),
        "pool_scale": 1.0 + nrm(ks[15], (L, BW), 0.02),
        "conv_w": nrm(ks[16], (L, CONV_W, BW), CONV_W ** -0.5),
        "conv_b": nrm(ks[17], (L, BW), 0.02),
        "conv_ln_g": 1.0 + nrm(ks[18], (L, BW), 0.02),
        "conv_ln_b": nrm(ks[19], (L, BW), 0.02),
    }


def reference(x, ln_g, ln_b, ffn_w_in, ffn_w_out, mix_w_in, gate_w, gate_b, branch_w, out_w,
              sg_ln_g, sg_ln_b, sg_w, sg_b, pool_w, pool_scale, conv_w, conv_b, conv_ln_g, conv_ln_b):
    for l in range(DEPTH):
        x = layer_norm(DN_ALPHA * x + 0.5 * swiglu_ffn(x, ffn_w_in[l, 0], ffn_w_out[l, 0]),
                       ln_g[l, 0], ln_b[l, 0])
        m = hybrid_mixer(x, mix_w_in[l], gate_w[l], gate_b[l], branch_w[l], out_w[l],
                         sg_ln_g[l], sg_ln_b[l], sg_w[l], sg_b[l], pool_w[l], pool_scale[l],
                         conv_w[l], conv_b[l], conv_ln_g[l], conv_ln_b[l])
        x = layer_norm(DN_ALPHA * x + m, ln_g[l, 1], ln_b[l, 1])
        x = layer_norm(DN_ALPHA * x + 0.5 * swiglu_ffn(x, ffn_w_in[l, 1], ffn_w_out[l, 1]),
                       ln_g[l, 2], ln_b[l, 2])
    return x
```

```python
import functools
import math

import jax
import jax.numpy as jnp
from jax import lax
from jax.experimental import pallas as pl
from jax.experimental.pallas import tpu as pltpu

D_MODEL = 1024
DEPTH = 2
BRANCH_W = D_MODEL // 4
HEAD_DIM = 64
SG_CHUNK = 128
SG_GROUPS = 4
POOL_WINDOWS = (2, 4, 8, 16)
CONV_W = 31
D_FF = 2816
IN_COLS = 8 * BRANCH_W
LN_EPS = 1e-5
DN_ALPHA = (2.0 * DEPTH) ** 0.25

LANES = 128
VMEM_LIMIT_BYTES = 56 * 1024 * 1024

KEY_BLOCK = 128
POOL_HALO = 16
CONV_HALO = 32

F32 = jnp.float32
BF16 = jnp.bfloat16


def _dot(a, b):
    return jnp.dot(a, b, preferred_element_type=F32)


def _layer_norm(y, g, b):
    mu = jnp.mean(y, axis=-1, keepdims=True)
    d = y - mu
    var = jnp.mean(d * d, axis=-1, keepdims=True)
    return d * lax.rsqrt(var + LN_EPS) * g + b


def _resident(shape):
    zeros = (0,) * len(shape)
    return pl.BlockSpec(shape, lambda *_: zeros, pipeline_mode=pl.Buffered(1))


def _ffn_kernel(x_ref, g_ref, b_ref, win_ref, wout_ref, o_ref, *, tf):
    x = x_ref[...]
    xb = x.astype(BF16)
    acc = None
    for c in range(D_FF // tf):
        gate = _dot(xb, win_ref[:, c * tf:(c + 1) * tf])
        up = _dot(xb, win_ref[:, D_FF + c * tf:D_FF + (c + 1) * tf])
        h = (gate * jax.nn.sigmoid(gate) * up).astype(BF16)
        part = _dot(h, wout_ref[c * tf:(c + 1) * tf, :])
        acc = part if acc is None else acc + part
    y = DN_ALPHA * x + 0.5 * acc
    o_ref[...] = _layer_norm(y, g_ref[...], b_ref[...])


def _ffn(x2, ln_g, ln_b, w_in, w_out, *, tm=512, tf=256):
    n = x2.shape[0]
    return pl.pallas_call(
        functools.partial(_ffn_kernel, tf=tf),
        grid=(n // tm,),
        in_specs=[
            pl.BlockSpec((tm, D_MODEL), lambda i: (i, 0)),
            _resident((1, D_MODEL)),
            _resident((1, D_MODEL)),
            _resident((D_MODEL, 2 * D_FF)),
            _resident((D_FF, D_MODEL)),
        ],
        out_specs=pl.BlockSpec((tm, D_MODEL), lambda i: (i, 0)),
        out_shape=jax.ShapeDtypeStruct((n, D_MODEL), F32),
        compiler_params=pltpu.CompilerParams(
            dimension_semantics=("arbitrary",), vmem_limit_bytes=VMEM_LIMIT_BYTES),
        name="ffn",
    )(x2, ln_g, ln_b, w_in, w_out)


def _gelu_tanh(x):
    c = math.sqrt(2.0 / math.pi)
    return x * (0.5 * (1.0 + jnp.tanh(c * (x + 0.044715 * (x * x * x)))))


def _mix_proj_kernel(x_ref, win_ref, sg_g_ref, sg_b_ref, sgw_ref, sgbias_ref, poolw_ref, pscale_ref,
                     convw_ref, convb_ref, cln_g_ref, cln_b_ref,
                     q_ref, kt_ref, v_ref, y_ref,
                     h_ref, pbuf, ybuf, *, ts):
    i = pl.program_id(1)
    rc = SG_CHUNK
    n_chunks = ts // rc

    xb = x_ref[...].astype(BF16)
    for c0 in range(0, IN_COLS, 512):
        h_ref[:, c0:c0 + 512] = _dot(xb, win_ref[:, c0:c0 + 512])

    @pl.when(i == 0)
    def _():
        pbuf[0:POOL_HALO, :] = jnp.zeros((POOL_HALO, BRANCH_W), F32)
        ybuf[0:CONV_HALO, :] = jnp.zeros((CONV_HALO, BRANCH_W), F32)

    tri = (lax.broadcasted_iota(jnp.int32, (SG_CHUNK, SG_CHUNK), 0)
           >= lax.broadcasted_iota(jnp.int32, (SG_CHUNK, SG_CHUNK), 1))
    w_stack = jnp.concatenate(
        [jnp.where(tri, sgw_ref[g], 0.0).astype(BF16) for g in range(SG_GROUPS)], axis=0)

    lane256 = lax.broadcasted_iota(jnp.int32, (rc, BRANCH_W), 1)
    lane_group = lane256 // (BRANCH_W // SG_GROUPS)
    lane128 = lax.broadcasted_iota(jnp.int32, (rc, LANES), 1)
    low_half = lane128 < (LANES // 2)
    row_f = lax.broadcasted_iota(jnp.int32, (rc, LANES), 0)

    for c in range(n_chunks):
        r0 = c * rc
        rows = slice(r0, r0 + rc)

        q_ref[rows, :] = (h_ref[rows, 0:BRANCH_W] * (HEAD_DIM ** -0.5)).astype(BF16)
        kt_ref[c] = h_ref[rows, BRANCH_W:2 * BRANCH_W].T.astype(BF16)
        v_ref[rows, :] = h_ref[rows, 2 * BRANCH_W:3 * BRANCH_W].astype(BF16)

        u = _gelu_tanh(h_ref[rows, 3 * BRANCH_W:4 * BRANCH_W])
        vv = _gelu_tanh(h_ref[rows, 4 * BRANCH_W:5 * BRANCH_W])
        vn = _layer_norm(vv, sg_g_ref[...], sg_b_ref[...]).astype(BF16)
        r = _dot(w_stack, vn)
        mixed = r[(SG_GROUPS - 1) * rc:SG_GROUPS * rc]
        for g in range(SG_GROUPS - 2, -1, -1):
            mixed = jnp.where(lane_group == g, r[g * rc:(g + 1) * rc], mixed)
        y_ref[rows, 0:BRANCH_W] = (u * (mixed + sgbias_ref[...])).astype(BF16)

        p = h_ref[rows, 5 * BRANCH_W:6 * BRANCH_W]
        pbuf[POOL_HALO + r0:POOL_HALO + r0 + rc, :] = p
        pos1 = (i * ts + r0 + 1 + row_f).astype(F32)
        pooled = []
        for half, (w_lo, w_hi) in enumerate(((POOL_WINDOWS[0], POOL_WINDOWS[1]),
                                             (POOL_WINDOWS[2], POOL_WINDOWS[3]))):
            cols = slice(half * LANES, (half + 1) * LANES)
            s_lo = pbuf[pl.ds(POOL_HALO + r0, rc), cols]
            for k in range(1, w_lo):
                s_lo = s_lo + pbuf[pl.ds(POOL_HALO + r0 - k, rc), cols]
            s_hi = s_lo
            for k in range(w_lo, w_hi):
                s_hi = s_hi + pbuf[pl.ds(POOL_HALO + r0 - k, rc), cols]
            cnt = jnp.minimum(pos1, jnp.where(low_half, float(w_lo), float(w_hi)))
            pooled.append(jnp.where(low_half, s_lo, s_hi) / cnt - p[:, cols])
        pooled = jnp.concatenate(pooled, axis=1).astype(BF16)
        y_ref[rows, BRANCH_W:2 * BRANCH_W] = (_dot(pooled, poolw_ref[...]) * pscale_ref[...]).astype(BF16)

        a = h_ref[rows, 6 * BRANCH_W:7 * BRANCH_W]
        gl = h_ref[rows, 7 * BRANCH_W:8 * BRANCH_W]
        ybuf[CONV_HALO + r0:CONV_HALO + r0 + rc, :] = a * jax.nn.sigmoid(gl)
        conv = None
        for j in range(CONV_W):
            tap = ybuf[pl.ds(CONV_HALO + r0 - (CONV_W - 1) + j, rc), :] * convw_ref[j:j + 1, :]
            conv = tap if conv is None else conv + tap
        cn = _layer_norm(conv + convb_ref[...], cln_g_ref[...], cln_b_ref[...])
        y_ref[rows, 2 * BRANCH_W:3 * BRANCH_W] = (cn * jax.nn.sigmoid(cn)).astype(BF16)

    pbuf[0:POOL_HALO, :] = pbuf[ts:ts + POOL_HALO, :]
    ybuf[0:CONV_HALO, :] = ybuf[ts:ts + CONV_HALO, :]


def _mix_proj(x, w_in, sg_g, sg_b, sg_w, sg_bias, pool_w, pool_scale, conv_w, conv_b, cln_g, cln_b, *, ts=512):
    b, s, _ = x.shape
    nkb = ts // KEY_BLOCK
    out_shape = (
        jax.ShapeDtypeStruct((b, s, BRANCH_W), BF16),
        jax.ShapeDtypeStruct((b, s // KEY_BLOCK, BRANCH_W, KEY_BLOCK), BF16),
        jax.ShapeDtypeStruct((b, s, BRANCH_W), BF16),
        jax.ShapeDtypeStruct((b, s, 3 * BRANCH_W), BF16),
    )
    return pl.pallas_call(
        functools.partial(_mix_proj_kernel, ts=ts),
        grid=(b, s // ts),
        in_specs=[
            pl.BlockSpec((None, ts, D_MODEL), lambda bi, i: (bi, i, 0)),
            _resident((D_MODEL, IN_COLS)),
            _resident((1, BRANCH_W)),
            _resident((1, BRANCH_W)),
            _resident((SG_GROUPS, SG_CHUNK, SG_CHUNK)),
            _resident((SG_CHUNK, BRANCH_W)),
            _resident((BRANCH_W, BRANCH_W)),
            _resident((1, BRANCH_W)),
            _resident((CONV_HALO, BRANCH_W)),
            _resident((1, BRANCH_W)),
            _resident((1, BRANCH_W)),
            _resident((1, BRANCH_W)),
        ],
        out_specs=(
            pl.BlockSpec((None, ts, BRANCH_W), lambda bi, i: (bi, i, 0)),
            pl.BlockSpec((None, nkb, BRANCH_W, KEY_BLOCK), lambda bi, i: (bi, i, 0, 0)),
            pl.BlockSpec((None, ts, BRANCH_W), lambda bi, i: (bi, i, 0)),
            pl.BlockSpec((None, ts, 3 * BRANCH_W), lambda bi, i: (bi, i, 0)),
        ),
        out_shape=out_shape,
        scratch_shapes=[
            pltpu.VMEM((ts, IN_COLS), F32),
            pltpu.VMEM((ts + POOL_HALO, BRANCH_W), F32),
            pltpu.VMEM((ts + CONV_HALO, BRANCH_W), F32),
        ],
        compiler_params=pltpu.CompilerParams(
            dimension_semantics=("arbitrary", "arbitrary"), vmem_limit_bytes=VMEM_LIMIT_BYTES),
        name="mix_proj",
    )(x, w_in, sg_g, sg_b, sg_w, sg_bias, pool_w, pool_scale, conv_w, conv_b, cln_g, cln_b)


def _sb_attn_kernel(q_ref, kt_ref, v_ref, u_ref, o_ref, q2_ref, carry_ref, acc_ref):
    i = pl.program_id(1)
    tq = KEY_BLOCK
    n_pairs = BRANCH_W // LANES
    lane = lax.broadcasted_iota(jnp.int32, (tq, LANES), 1)
    low = lane < HEAD_DIM
    row2 = lax.broadcasted_iota(jnp.int32, (2 * tq, KEY_BLOCK), 0)
    col2 = lax.broadcasted_iota(jnp.int32, (2 * tq, KEY_BLOCK), 1)
    causal2 = col2 < jnp.where(row2 >= tq, row2 - tq, row2)

    for p in range(n_pairs):
        qp = q_ref[:, p * LANES:(p + 1) * LANES]
        zero = jnp.zeros_like(qp)
        q2_ref[p] = jnp.concatenate([jnp.where(low, qp, zero), jnp.where(low, zero, qp)], axis=0)

    def block(p, kb, diag):
        kt = kt_ref[kb, p * LANES:(p + 1) * LANES, :]
        z = _dot(q2_ref[p], kt)
        lk = jnp.minimum(-z, 0.0) - jnp.log(1.0 + jnp.exp(-jnp.abs(z)))
        if diag:
            lk = jnp.where(causal2, lk, 0.0)
        hi = lk.astype(BF16)
        lo = (lk - hi.astype(F32)).astype(BF16)
        res = _dot(jnp.concatenate([hi, lo], axis=1), u_ref[...])
        suffix = res[:, 0:KEY_BLOCK]
        total = res[:, KEY_BLOCK:2 * KEY_BLOCK]
        a = z + suffix
        if not diag:
            a = a + carry_ref[p]
        w = jnp.exp(a)
        if diag:
            w = jnp.where(causal2, w, 0.0)
        wb = w.astype(BF16)
        wcat = jnp.concatenate([wb[0:tq], wb[tq:2 * tq]], axis=1)
        vp = v_ref[pl.ds(pl.multiple_of(kb * KEY_BLOCK, KEY_BLOCK), KEY_BLOCK), p * LANES:(p + 1) * LANES]
        vzero = jnp.zeros_like(vp)
        v2 = jnp.concatenate([jnp.where(low, vp, vzero), jnp.where(low, vzero, vp)], axis=0)
        pv = _dot(wcat, v2)
        if diag:
            acc_ref[p] = pv
            carry_ref[p] = total
        else:
            acc_ref[p] = acc_ref[p] + pv
            carry_ref[p] = carry_ref[p] + total

    for p in range(n_pairs):
        block(p, i, True)

    def body(j, _):
        kb = i - 1 - j
        for p in range(n_pairs):
            block(p, kb, False)
        return 0

    lax.fori_loop(0, i, body, 0)

    for p in range(n_pairs):
        o_ref[:, p * LANES:(p + 1) * LANES] = acc_ref[p].astype(BF16)


def _sb_attn(q, kt, v, u_ext):
    b, s, _ = q.shape
    tq = KEY_BLOCK
    n_pairs = BRANCH_W // LANES
    return pl.pallas_call(
        _sb_attn_kernel,
        grid=(b, s // tq),
        in_specs=[
            pl.BlockSpec((None, tq, BRANCH_W), lambda bi, i: (bi, i, 0)),
            pl.BlockSpec((None, s // KEY_BLOCK, BRANCH_W, KEY_BLOCK), lambda bi, i: (bi, 0, 0, 0)),
            pl.BlockSpec((None, s, BRANCH_W), lambda bi, i: (bi, 0, 0)),
            _resident((2 * KEY_BLOCK, 2 * KEY_BLOCK)),
        ],
        out_specs=pl.BlockSpec((None, tq, BRANCH_W), lambda bi, i: (bi, i, 0)),
        out_shape=jax.ShapeDtypeStruct((b, s, BRANCH_W), BF16),
        scratch_shapes=[
            pltpu.VMEM((n_pairs, 2 * tq, LANES), BF16),
            pltpu.VMEM((n_pairs, 2 * tq, KEY_BLOCK), F32),
            pltpu.VMEM((n_pairs, tq, LANES), F32),
        ],
        compiler_params=pltpu.CompilerParams(
            dimension_semantics=("arbitrary", "arbitrary"), vmem_limit_bytes=VMEM_LIMIT_BYTES),
        name="sb_attn",
    )(q, kt, v, u_ext)


def _mix_merge_kernel(x_ref, ya_ref, y_ref, gw_ref, gb_ref, bw_ref, ow_ref, g_ref, b_ref, o_ref, m_ref, *, tn):
    x = x_ref[...]
    xb = x.astype(BF16)
    ys = [ya_ref[...]] + [y_ref[:, k * BRANCH_W:(k + 1) * BRANCH_W] for k in range(3)]
    for c0 in range(0, D_MODEL, tn):
        cols = slice(c0, c0 + tn)
        merged = None
        for n in range(4):
            gate = jax.nn.sigmoid(_dot(xb, gw_ref[n, :, cols]) + gb_ref[n:n + 1, cols])
            term = gate * _dot(ys[n], bw_ref[n, :, cols])
            merged = term if merged is None else merged + term
        m_ref[:, cols] = merged.astype(BF16)
    y = DN_ALPHA * x + _dot(m_ref[...], ow_ref[...])
    o_ref[...] = _layer_norm(y, g_ref[...], b_ref[...])


def _mix_merge(x2, ya, ybcd, gate_w, gate_b, branch_w, out_w, ln_g, ln_b, *, tm=512, tn=256):
    n = x2.shape[0]
    return pl.pallas_call(
        functools.partial(_mix_merge_kernel, tn=tn),
        grid=(n // tm,),
        in_specs=[
            pl.BlockSpec((tm, D_MODEL), lambda i: (i, 0)),
            pl.BlockSpec((tm, BRANCH_W), lambda i: (i, 0)),
            pl.BlockSpec((tm, 3 * BRANCH_W), lambda i: (i, 0)),
            _resident((4, D_MODEL, D_MODEL)),
            _resident((4, D_MODEL)),
            _resident((4, BRANCH_W, D_MODEL)),
            _resident((D_MODEL, D_MODEL)),
            _resident((1, D_MODEL)),
            _resident((1, D_MODEL)),
        ],
        out_specs=pl.BlockSpec((tm, D_MODEL), lambda i: (i, 0)),
        out_shape=jax.ShapeDtypeStruct((n, D_MODEL), F32),
        scratch_shapes=[pltpu.VMEM((tm, D_MODEL), BF16)],
        compiler_params=pltpu.CompilerParams(
            dimension_semantics=("arbitrary",), vmem_limit_bytes=VMEM_LIMIT_BYTES),
        name="mix_merge",
    )(x2, ya, ybcd, gate_w, gate_b, branch_w, out_w, ln_g, ln_b)


def _suffix_sum_matrix():
    j = jnp.arange(KEY_BLOCK)[:, None]
    s = jnp.arange(KEY_BLOCK)[None, :]
    half = jnp.concatenate([(j >= s).astype(BF16), jnp.ones((KEY_BLOCK, KEY_BLOCK), BF16)], axis=1)
    return jnp.concatenate([half, half], axis=0)


def _block_diag(w):
    g, r, c = w.shape
    out = jnp.zeros((g * r, g * c), w.dtype)
    for k in range(g):
        out = out.at[k * r:(k + 1) * r, k * c:(k + 1) * c].set(w[k])
    return out


def kernel(x, ln_g, ln_b, ffn_w_in, ffn_w_out, mix_w_in, gate_w, gate_b, branch_w, out_w, sg_ln_g, sg_ln_b, sg_w,
           sg_b, pool_w, pool_scale, conv_w, conv_b, conv_ln_g, conv_ln_b):
    b, s, d = x.shape
    n = b * s
    u_ext = _suffix_sum_matrix()
    row = lambda a: a.reshape(1, -1)
    x2 = x.reshape(n, d)
    for l in range(DEPTH):
        x2 = _ffn(x2, row(ln_g[l, 0]), row(ln_b[l, 0]), ffn_w_in[l, 0].astype(BF16), ffn_w_out[l, 0].astype(BF16))
        sg_bias = jnp.repeat(sg_b[l].T, BRANCH_W // SG_GROUPS, axis=1)
        conv_w_pad = jnp.pad(conv_w[l], ((0, CONV_HALO - CONV_W), (0, 0)))
        q, kt, v, ybcd = _mix_proj(
            x2.reshape(b, s, d), mix_w_in[l].astype(BF16), row(sg_ln_g[l]), row(sg_ln_b[l]), sg_w[l], sg_bias,
            _block_diag(pool_w[l]).astype(BF16), row(pool_scale[l]), conv_w_pad, row(conv_b[l]),
            row(conv_ln_g[l]), row(conv_ln_b[l]))
        ya = _sb_attn(q, kt, v, u_ext)
        x2 = _mix_merge(x2, ya.reshape(n, BRANCH_W), ybcd.reshape(n, 3 * BRANCH_W), gate_w[l].astype(BF16),
                        gate_b[l], branch_w[l].astype(BF16), out_w[l].astype(BF16), row(ln_g[l, 1]), row(ln_b[l, 1]))
        x2 = _ffn(x2, row(ln_g[l, 2]), row(ln_b[l, 2]), ffn_w_in[l, 1].astype(BF16), ffn_w_out[l, 1].astype(BF16))
    return x2.reshape(b, s, d)
```

```python
import functools
import math

import jax
import jax.numpy as jnp
from jax import lax
from jax.experimental import pallas as pl
from jax.experimental.pallas import tpu as pltpu

D_MODEL = 1024
DEPTH = 2
BRANCH_W = D_MODEL // 4
HEAD_DIM = 64
SG_CHUNK = 128
SG_GROUPS = 4
POOL_WINDOWS = (2, 4, 8, 16)
CONV_W = 31
D_FF = 2816
IN_COLS = 8 * BRANCH_W
LN_EPS = 1e-5
DN_ALPHA = (2.0 * DEPTH) ** 0.25
LOG2E = 1.0 / math.log(2.0)

LANES = 128
VMEM_LIMIT_BYTES = 56 * 1024 * 1024

KEY_BLOCK = 128
POOL_HALO = 16
CONV_HALO = 32

F32 = jnp.float32
BF16 = jnp.bfloat16


def _dot(a, b):
    return jnp.dot(a, b, preferred_element_type=F32)


def _layer_norm(y, g, b):
    mu = jnp.mean(y, axis=-1, keepdims=True)
    d = y - mu
    var = jnp.mean(d * d, axis=-1, keepdims=True)
    return d * lax.rsqrt(var + LN_EPS) * g + b


def _resident(shape):
    zeros = (0,) * len(shape)
    return pl.BlockSpec(shape, lambda *_: zeros, pipeline_mode=pl.Buffered(1))


def _ffn_kernel(x_ref, g_ref, b_ref, win_ref, wout_ref, o_ref, *, tf):
    x = x_ref[...]
    xb = x.astype(BF16)
    acc = None
    for c in range(D_FF // tf):
        gate = _dot(xb, win_ref[:, c * tf:(c + 1) * tf])
        up = _dot(xb, win_ref[:, D_FF + c * tf:D_FF + (c + 1) * tf])
        h = (gate * jax.nn.sigmoid(gate) * up).astype(BF16)
        part = _dot(h, wout_ref[c * tf:(c + 1) * tf, :])
        acc = part if acc is None else acc + part
    y = DN_ALPHA * x + 0.5 * acc
    o_ref[...] = _layer_norm(y, g_ref[...], b_ref[...])


def _ffn(x2, ln_g, ln_b, w_in, w_out, *, tm=512, tf=256):
    n = x2.shape[0]
    return pl.pallas_call(
        functools.partial(_ffn_kernel, tf=tf),
        grid=(n // tm,),
        in_specs=[
            pl.BlockSpec((tm, D_MODEL), lambda i: (i, 0)),
            _resident((1, D_MODEL)),
            _resident((1, D_MODEL)),
            _resident((D_MODEL, 2 * D_FF)),
            _resident((D_FF, D_MODEL)),
        ],
        out_specs=pl.BlockSpec((tm, D_MODEL), lambda i: (i, 0)),
        out_shape=jax.ShapeDtypeStruct((n, D_MODEL), F32),
        compiler_params=pltpu.CompilerParams(
            dimension_semantics=("arbitrary",), vmem_limit_bytes=VMEM_LIMIT_BYTES),
        name="ffn",
    )(x2, ln_g, ln_b, w_in, w_out)


def _gelu_tanh(x):
    c = math.sqrt(2.0 / math.pi)
    return x * (0.5 * (1.0 + jnp.tanh(c * (x + 0.044715 * (x * x * x)))))


def _mix_proj_kernel(x_ref, win_ref, sg_g_ref, sg_b_ref, sgw_ref, sgbias_ref, poolw_ref, pscale_ref,
                     convw_ref, convb_ref, cln_g_ref, cln_b_ref,
                     q_ref, kt_ref, v_ref, y_ref,
                     h_ref, pbuf, ybuf, *, ts):
    i = pl.program_id(1)
    rc = SG_CHUNK
    n_chunks = ts // rc

    xb = x_ref[...].astype(BF16)
    for c0 in range(0, IN_COLS, 512):
        h_ref[:, c0:c0 + 512] = _dot(xb, win_ref[:, c0:c0 + 512])

    @pl.when(i == 0)
    def _():
        pbuf[0:POOL_HALO, :] = jnp.zeros((POOL_HALO, BRANCH_W), F32)
        ybuf[0:CONV_HALO, :] = jnp.zeros((CONV_HALO, BRANCH_W), F32)

    tri = (lax.broadcasted_iota(jnp.int32, (SG_CHUNK, SG_CHUNK), 0)
           >= lax.broadcasted_iota(jnp.int32, (SG_CHUNK, SG_CHUNK), 1))
    w_stack = jnp.concatenate(
        [jnp.where(tri, sgw_ref[g], 0.0).astype(BF16) for g in range(SG_GROUPS)], axis=0)

    lane256 = lax.broadcasted_iota(jnp.int32, (rc, BRANCH_W), 1)
    lane_group = lane256 // (BRANCH_W // SG_GROUPS)
    lane128 = lax.broadcasted_iota(jnp.int32, (rc, LANES), 1)
    low_half = lane128 < (LANES // 2)
    row_f = lax.broadcasted_iota(jnp.int32, (rc, LANES), 0)

    for c in range(n_chunks):
        r0 = c * rc
        rows = slice(r0, r0 + rc)

        q_ref[rows, :] = (h_ref[rows, 0:BRANCH_W] * (HEAD_DIM ** -0.5 * LOG2E)).astype(BF16)
        kt_ref[c] = h_ref[rows, BRANCH_W:2 * BRANCH_W].T.astype(BF16)
        v_ref[rows, :] = h_ref[rows, 2 * BRANCH_W:3 * BRANCH_W].astype(BF16)

        u = _gelu_tanh(h_ref[rows, 3 * BRANCH_W:4 * BRANCH_W])
        vv = _gelu_tanh(h_ref[rows, 4 * BRANCH_W:5 * BRANCH_W])
        vn = _layer_norm(vv, sg_g_ref[...], sg_b_ref[...]).astype(BF16)
        r = _dot(w_stack, vn)
        mixed = r[(SG_GROUPS - 1) * rc:SG_GROUPS * rc]
        for g in range(SG_GROUPS - 2, -1, -1):
            mixed = jnp.where(lane_group == g, r[g * rc:(g + 1) * rc], mixed)
        y_ref[rows, 0:BRANCH_W] = (u * (mixed + sgbias_ref[...])).astype(BF16)

        p = h_ref[rows, 5 * BRANCH_W:6 * BRANCH_W]
        pbuf[POOL_HALO + r0:POOL_HALO + r0 + rc, :] = p
        pos1 = (i * ts + r0 + 1 + row_f).astype(F32)
        pooled = []
        for half, (w_lo, w_hi) in enumerate(((POOL_WINDOWS[0], POOL_WINDOWS[1]),
                                             (POOL_WINDOWS[2], POOL_WINDOWS[3]))):
            cols = slice(half * LANES, (half + 1) * LANES)
            s_lo = pbuf[pl.ds(POOL_HALO + r0, rc), cols]
            for k in range(1, w_lo):
                s_lo = s_lo + pbuf[pl.ds(POOL_HALO + r0 - k, rc), cols]
            s_hi = s_lo
            for k in range(w_lo, w_hi):
                s_hi = s_hi + pbuf[pl.ds(POOL_HALO + r0 - k, rc), cols]
            cnt = jnp.minimum(pos1, jnp.where(low_half, float(w_lo), float(w_hi)))
            pooled.append(jnp.where(low_half, s_lo, s_hi) / cnt - p[:, cols])
        pooled = jnp.concatenate(pooled, axis=1).astype(BF16)
        y_ref[rows, BRANCH_W:2 * BRANCH_W] = (_dot(pooled, poolw_ref[...]) * pscale_ref[...]).astype(BF16)

        a = h_ref[rows, 6 * BRANCH_W:7 * BRANCH_W]
        gl = h_ref[rows, 7 * BRANCH_W:8 * BRANCH_W]
        ybuf[CONV_HALO + r0:CONV_HALO + r0 + rc, :] = a * jax.nn.sigmoid(gl)
        conv = None
        for j in range(CONV_W):
            tap = ybuf[pl.ds(CONV_HALO + r0 - (CONV_W - 1) + j, rc), :] * convw_ref[j:j + 1, :]
            conv = tap if conv is None else conv + tap
        cn = _layer_norm(conv + convb_ref[...], cln_g_ref[...], cln_b_ref[...])
        y_ref[rows, 2 * BRANCH_W:3 * BRANCH_W] = (cn * jax.nn.sigmoid(cn)).astype(BF16)

    pbuf[0:POOL_HALO, :] = pbuf[ts:ts + POOL_HALO, :]
    ybuf[0:CONV_HALO, :] = ybuf[ts:ts + CONV_HALO, :]


def _mix_proj(x, w_in, sg_g, sg_b, sg_w, sg_bias, pool_w, pool_scale, conv_w, conv_b, cln_g, cln_b, *, ts=512):
    b, s, _ = x.shape
    nkb = ts // KEY_BLOCK
    out_shape = (
        jax.ShapeDtypeStruct((b, s, BRANCH_W), BF16),
        jax.ShapeDtypeStruct((b, s // KEY_BLOCK, BRANCH_W, KEY_BLOCK), BF16),
        jax.ShapeDtypeStruct((b, s, BRANCH_W), BF16),
        jax.ShapeDtypeStruct((b, s, 3 * BRANCH_W), BF16),
    )
    return pl.pallas_call(
        functools.partial(_mix_proj_kernel, ts=ts),
        grid=(b, s // ts),
        in_specs=[
            pl.BlockSpec((None, ts, D_MODEL), lambda bi, i: (bi, i, 0)),
            _resident((D_MODEL, IN_COLS)),
            _resident((1, BRANCH_W)),
            _resident((1, BRANCH_W)),
            _resident((SG_GROUPS, SG_CHUNK, SG_CHUNK)),
            _resident((SG_CHUNK, BRANCH_W)),
            _resident((BRANCH_W, BRANCH_W)),
            _resident((1, BRANCH_W)),
            _resident((CONV_HALO, BRANCH_W)),
            _resident((1, BRANCH_W)),
            _resident((1, BRANCH_W)),
            _resident((1, BRANCH_W)),
        ],
        out_specs=(
            pl.BlockSpec((None, ts, BRANCH_W), lambda bi, i: (bi, i, 0)),
            pl.BlockSpec((None, nkb, BRANCH_W, KEY_BLOCK), lambda bi, i: (bi, i, 0, 0)),
            pl.BlockSpec((None, ts, BRANCH_W), lambda bi, i: (bi, i, 0)),
            pl.BlockSpec((None, ts, 3 * BRANCH_W), lambda bi, i: (bi, i, 0)),
        ),
        out_shape=out_shape,
        scratch_shapes=[
            pltpu.VMEM((ts, IN_COLS), F32),
            pltpu.VMEM((ts + POOL_HALO, BRANCH_W), F32),
            pltpu.VMEM((ts + CONV_HALO, BRANCH_W), F32),
        ],
        compiler_params=pltpu.CompilerParams(
            dimension_semantics=("arbitrary", "arbitrary"), vmem_limit_bytes=VMEM_LIMIT_BYTES),
        name="mix_proj",
    )(x, w_in, sg_g, sg_b, sg_w, sg_bias, pool_w, pool_scale, conv_w, conv_b, cln_g, cln_b)


SB_TQ = 256


def _sb_attn_kernel(q_ref, kt_ref, v_ref, u_ref, o_ref,
                    q2_ref, z_ref, hl_ref, tot_ref, w_ref, carry_ref, acc_ref):
    i = pl.program_id(1)
    tq = SB_TQ
    n_pairs = BRANCH_W // LANES
    kb_max = i * (tq // KEY_BLOCK) + (tq // KEY_BLOCK - 1)
    low = lax.broadcasted_iota(jnp.int32, (KEY_BLOCK, LANES), 1) < HEAD_DIM
    low_q = lax.broadcasted_iota(jnp.int32, (tq, LANES), 1) < HEAD_DIM

    for p in range(n_pairs):
        qp = q_ref[:, p * LANES:(p + 1) * LANES]
        zero = jnp.zeros_like(qp)
        q2_ref[p] = jnp.concatenate([jnp.where(low_q, qp, zero), jnp.where(low_q, zero, qp)], axis=0)

    z_ref[1] = jnp.zeros(z_ref.shape[1:], F32)
    hl_ref[1] = jnp.zeros(hl_ref.shape[1:], BF16)
    tot_ref[1] = jnp.zeros(tot_ref.shape[1:], F32)
    w_ref[0] = jnp.zeros(w_ref.shape[1:], BF16)
    carry_ref[...] = jnp.zeros(carry_ref.shape, F32)
    acc_ref[...] = jnp.zeros(acc_ref.shape, F32)

    def causal(kb):
        row = lax.broadcasted_iota(jnp.int32, (2 * tq, KEY_BLOCK), 0)
        col = lax.broadcasted_iota(jnp.int32, (2 * tq, KEY_BLOCK), 1)
        qpos = i * tq + jnp.where(row >= tq, row - tq, row)
        return kb * KEY_BLOCK + col < qpos

    def iteration(t, s, masked):
        o = 1 - s
        kb1 = jnp.maximum(kb_max - t, 0)
        kb2 = kb_max - t + 1
        kb3 = jnp.clip(kb_max - t + 2, 0, kb_max)
        v_rows = pl.ds(pl.multiple_of(kb3 * KEY_BLOCK, KEY_BLOCK), KEY_BLOCK)

        zs, rs, pvs = [], [], []
        for p in range(n_pairs):
            zs.append(_dot(q2_ref[p], kt_ref[kb1, p * LANES:(p + 1) * LANES, :]))
        for p in range(n_pairs):
            rs.append(_dot(hl_ref[o, p], u_ref[...]))
        for p in range(n_pairs):
            vp = v_ref[v_rows, p * LANES:(p + 1) * LANES]
            vzero = jnp.zeros_like(vp)
            v2 = jnp.concatenate([jnp.where(low, vp, vzero), jnp.where(low, vzero, vp)], axis=0)
            pvs.append(_dot(w_ref[s, p], v2))

        for p in range(n_pairs):
            z = zs[p]
            nz = -z
            lk = jnp.minimum(nz, 0.0) - jnp.log(1.0 + jnp.exp2(jnp.minimum(z, nz))) * LOG2E
            if masked:
                lk = jnp.where(causal(kb1), lk, 0.0)
            hi = lk.astype(BF16)
            lo = (lk - hi.astype(F32)).astype(BF16)
            z_ref[s, p] = z
            hl_ref[s, p] = jnp.concatenate([hi, lo], axis=1)
            tot_ref[s, p] = jnp.broadcast_to(jnp.sum(lk, axis=1, keepdims=True), lk.shape)

        for p in range(n_pairs):
            w = jnp.exp2(z_ref[o, p] + rs[p] + carry_ref[p])
            if masked:
                w = jnp.where(causal(kb2), w, 0.0)
            wb = w.astype(BF16)
            w_ref[o, p] = jnp.concatenate([wb[0:tq], wb[tq:2 * tq]], axis=1)
            carry_ref[p] = carry_ref[p] + tot_ref[o, p]

        for p in range(n_pairs):
            acc_ref[p] = acc_ref[p] + pvs[p]

    def make_body(masked):
        def body(tt, _):
            iteration(2 * tt, 0, masked)
            iteration(2 * tt + 1, 1, masked)
            return 0
        return body

    lax.fori_loop(0, 2, make_body(True), 0)
    lax.fori_loop(2, i + 2, make_body(False), 0)

    for p in range(n_pairs):
        o_ref[:, p * LANES:(p + 1) * LANES] = acc_ref[p].astype(BF16)


def _sb_attn(q, kt, v, u_tri):
    b, s, _ = q.shape
    tq = SB_TQ
    n_pairs = BRANCH_W // LANES
    return pl.pallas_call(
        _sb_attn_kernel,
        grid=(b, s // tq),
        in_specs=[
            pl.BlockSpec((None, tq, BRANCH_W), lambda bi, i: (bi, i, 0)),
            pl.BlockSpec((None, s // KEY_BLOCK, BRANCH_W, KEY_BLOCK), lambda bi, i: (bi, 0, 0, 0)),
            pl.BlockSpec((None, s, BRANCH_W), lambda bi, i: (bi, 0, 0)),
            _resident((2 * KEY_BLOCK, KEY_BLOCK)),
        ],
        out_specs=pl.BlockSpec((None, tq, BRANCH_W), lambda bi, i: (bi, i, 0)),
        out_shape=jax.ShapeDtypeStruct((b, s, BRANCH_W), BF16),
        scratch_shapes=[
            pltpu.VMEM((n_pairs, 2 * tq, LANES), BF16),
            pltpu.VMEM((2, n_pairs, 2 * tq, KEY_BLOCK), F32),
            pltpu.VMEM((2, n_pairs, 2 * tq, 2 * KEY_BLOCK), BF16),
            pltpu.VMEM((2, n_pairs, 2 * tq, KEY_BLOCK), F32),
            pltpu.VMEM((2, n_pairs, tq, 2 * KEY_BLOCK), BF16),
            pltpu.VMEM((n_pairs, 2 * tq, KEY_BLOCK), F32),
            pltpu.VMEM((n_pairs, tq, LANES), F32),
        ],
        compiler_params=pltpu.CompilerParams(
            dimension_semantics=("arbitrary", "arbitrary"), vmem_limit_bytes=VMEM_LIMIT_BYTES),
        name="sb_attn",
    )(q, kt, v, u_tri)


def _mix_merge_kernel(x_ref, ya_ref, y_ref, gw_ref, gb_ref, bw_ref, ow_ref, g_ref, b_ref, o_ref, m_ref, *, tn):
    x = x_ref[...]
    xb = x.astype(BF16)
    ys = [ya_ref[...]] + [y_ref[:, k * BRANCH_W:(k + 1) * BRANCH_W] for k in range(3)]
    for c0 in range(0, D_MODEL, tn):
        cols = slice(c0, c0 + tn)
        merged = None
        for n in range(4):
            gate = jax.nn.sigmoid(_dot(xb, gw_ref[n, :, cols]) + gb_ref[n:n + 1, cols])
            term = gate * _dot(ys[n], bw_ref[n, :, cols])
            merged = term if merged is None else merged + term
        m_ref[:, cols] = merged.astype(BF16)
    y = DN_ALPHA * x + _dot(m_ref[...], ow_ref[...])
    o_ref[...] = _layer_norm(y, g_ref[...], b_ref[...])


def _mix_merge(x2, ya, ybcd, gate_w, gate_b, branch_w, out_w, ln_g, ln_b, *, tm=512, tn=256):
    n = x2.shape[0]
    return pl.pallas_call(
        functools.partial(_mix_merge_kernel, tn=tn),
        grid=(n // tm,),
        in_specs=[
            pl.BlockSpec((tm, D_MODEL), lambda i: (i, 0)),
            pl.BlockSpec((tm, BRANCH_W), lambda i: (i, 0)),
            pl.BlockSpec((tm, 3 * BRANCH_W), lambda i: (i, 0)),
            _resident((4, D_MODEL, D_MODEL)),
            _resident((4, D_MODEL)),
            _resident((4, BRANCH_W, D_MODEL)),
            _resident((D_MODEL, D_MODEL)),
            _resident((1, D_MODEL)),
            _resident((1, D_MODEL)),
        ],
        out_specs=pl.BlockSpec((tm, D_MODEL), lambda i: (i, 0)),
        out_shape=jax.ShapeDtypeStruct((n, D_MODEL), F32),
        scratch_shapes=[pltpu.VMEM((tm, D_MODEL), BF16)],
        compiler_params=pltpu.CompilerParams(
            dimension_semantics=("arbitrary",), vmem_limit_bytes=VMEM_LIMIT_BYTES),
        name="mix_merge",
    )(x2, ya, ybcd, gate_w, gate_b, branch_w, out_w, ln_g, ln_b)


def _suffix_sum_matrix():
    j = jnp.arange(KEY_BLOCK)[:, None]
    s = jnp.arange(KEY_BLOCK)[None, :]
    u = (j >= s).astype(BF16)
    return jnp.concatenate([u, u], axis=0)


def _block_diag(w):
    g, r, c = w.shape
    out = jnp.zeros((g * r, g * c), w.dtype)
    for k in range(g):
        out = out.at[k * r:(k + 1) * r, k * c:(k + 1) * c].set(w[k])
    return out


def kernel(x, ln_g, ln_b, ffn_w_in, ffn_w_out, mix_w_in, gate_w, gate_b, branch_w, out_w, sg_ln_g, sg_ln_b, sg_w,
           sg_b, pool_w, pool_scale, conv_w, conv_b, conv_ln_g, conv_ln_b):
    b, s, d = x.shape
    n = b * s
    u_ext = _suffix_sum_matrix()
    row = lambda a: a.reshape(1, -1)
    x2 = x.reshape(n, d)
    for l in range(DEPTH):
        x2 = _ffn(x2, row(ln_g[l, 0]), row(ln_b[l, 0]), ffn_w_in[l, 0].astype(BF16), ffn_w_out[l, 0].astype(BF16))
        sg_bias = jnp.repeat(sg_b[l].T, BRANCH_W // SG_GROUPS, axis=1)
        conv_w_pad = jnp.pad(conv_w[l], ((0, CONV_HALO - CONV_W), (0, 0)))
        q, kt, v, ybcd = _mix_proj(
            x2.reshape(b, s, d), mix_w_in[l].astype(BF16), row(sg_ln_g[l]), row(sg_ln_b[l]), sg_w[l], sg_bias,
            _block_diag(pool_w[l]).astype(BF16), row(pool_scale[l]), conv_w_pad, row(conv_b[l]),
            row(conv_ln_g[l]), row(conv_ln_b[l]))
        ya = _sb_attn(q, kt, v, u_ext)
        x2 = _mix_merge(x2, ya.reshape(n, BRANCH_W), ybcd.reshape(n, 3 * BRANCH_W), gate_w[l].astype(BF16),
                        gate_b[l], branch_w[l].astype(BF16), out_w[l].astype(BF16), row(ln_g[l, 1]), row(ln_b[l, 1]))
        x2 = _ffn(x2, row(ln_g[l, 2]), row(ln_b[l, 2]), ffn_w_in[l, 1].astype(BF16), ffn_w_out[l, 1].astype(BF16))
    return x2.reshape(b, s, d)
```

```python
import functools
import math

import jax
import jax.numpy as jnp
from jax import lax
from jax.experimental import pallas as pl
from jax.experimental.pallas import tpu as pltpu

D_MODEL = 1024
DEPTH = 2
BRANCH_W = D_MODEL // 4
HEAD_DIM = 64
SG_CHUNK = 128
SG_GROUPS = 4
POOL_WINDOWS = (2, 4, 8, 16)
CONV_W = 31
D_FF = 2816
IN_COLS = 8 * BRANCH_W
LN_EPS = 1e-5
DN_ALPHA = (2.0 * DEPTH) ** 0.25
LOG2E = 1.0 / math.log(2.0)

LANES = 128
SUBLANES = 8
VMEM_LIMIT_BYTES = 56 * 1024 * 1024

KEY_BLOCK = 128
POOL_HALO = 16
CONV_HALO = 32

F32 = jnp.float32
BF16 = jnp.bfloat16


def _dot(a, b):
    return jnp.dot(a, b, preferred_element_type=F32)


def _layer_norm(y, g, b):
    mu = jnp.mean(y, axis=-1, keepdims=True)
    d = y - mu
    var = jnp.mean(d * d, axis=-1, keepdims=True)
    return d * lax.rsqrt(var + LN_EPS) * g + b


def _resident(shape):
    zeros = (0,) * len(shape)
    return pl.BlockSpec(shape, lambda *_: zeros, pipeline_mode=pl.Buffered(1))


def _ffn_kernel(x_ref, g_ref, b_ref, win_ref, wout_ref, o_ref, *, tf, sub):
    for r0 in range(0, x_ref.shape[0], sub):
        rows = slice(r0, r0 + sub)
        x = x_ref[rows, :]
        xb = x.astype(BF16)
        acc = None
        for c in range(D_FF // tf):
            gate = _dot(xb, win_ref[:, c * tf:(c + 1) * tf])
            up = _dot(xb, win_ref[:, D_FF + c * tf:D_FF + (c + 1) * tf])
            h = (gate * jax.nn.sigmoid(gate) * up).astype(BF16)
            part = _dot(h, wout_ref[c * tf:(c + 1) * tf, :])
            acc = part if acc is None else acc + part
        y = DN_ALPHA * x + 0.5 * acc
        o_ref[rows, :] = _layer_norm(y, g_ref[...], b_ref[...])


def _ffn(x2, ln_g, ln_b, w_in, w_out, *, tm=1024, tf=256, sub=512):
    n = x2.shape[0]
    return pl.pallas_call(
        functools.partial(_ffn_kernel, tf=tf, sub=sub),
        grid=(n // tm,),
        in_specs=[
            pl.BlockSpec((tm, D_MODEL), lambda i: (i, 0)),
            _resident((1, D_MODEL)),
            _resident((1, D_MODEL)),
            _resident((D_MODEL, 2 * D_FF)),
            _resident((D_FF, D_MODEL)),
        ],
        out_specs=pl.BlockSpec((tm, D_MODEL), lambda i: (i, 0)),
        out_shape=jax.ShapeDtypeStruct((n, D_MODEL), F32),
        compiler_params=pltpu.CompilerParams(
            dimension_semantics=("arbitrary",), vmem_limit_bytes=VMEM_LIMIT_BYTES),
        name="ffn",
    )(x2, ln_g, ln_b, w_in, w_out)


def _gelu_tanh(x):
    c = math.sqrt(2.0 / math.pi)
    return x * (0.5 * (1.0 + jnp.tanh(c * (x + 0.044715 * (x * x * x)))))


def _mix_proj_kernel(x_ref, win_ref, sg_g_ref, sg_b_ref, sgw_ref, sgbias_ref, poolw_ref, pscale_ref,
                     convw_ref, convb_ref, cln_g_ref, cln_b_ref,
                     q_ref, kt_ref, v_ref, y_ref,
                     h_ref, pbuf, ybuf, pshift, cshift, *, ts):
    i = pl.program_id(1)
    rc = SG_CHUNK
    n_chunks = ts // rc

    @pl.when(i == 0)
    def _():
        pbuf[0:POOL_HALO, :] = jnp.zeros((POOL_HALO, BRANCH_W), F32)
        ybuf[0:CONV_HALO, :] = jnp.zeros((CONV_HALO, BRANCH_W), F32)

    for r0 in range(0, ts, ts // 2):
        xb = x_ref[r0:r0 + ts // 2, :].astype(BF16)
        for c0 in range(0, IN_COLS, 512):
            h_ref[r0:r0 + ts // 2, c0:c0 + 512] = _dot(xb, win_ref[:, c0:c0 + 512])

    tri = (lax.broadcasted_iota(jnp.int32, (SG_CHUNK, SG_CHUNK), 0)
           >= lax.broadcasted_iota(jnp.int32, (SG_CHUNK, SG_CHUNK), 1))
    w_stack = jnp.concatenate(
        [jnp.where(tri, sgw_ref[g], 0.0).astype(BF16) for g in range(SG_GROUPS)], axis=0)

    lane256 = lax.broadcasted_iota(jnp.int32, (rc, BRANCH_W), 1)
    lane_group = lane256 // (BRANCH_W // SG_GROUPS)
    lane128 = lax.broadcasted_iota(jnp.int32, (rc, LANES), 1)
    low_half = lane128 < (LANES // 2)
    row_f = lax.broadcasted_iota(jnp.int32, (rc, LANES), 0)

    for c in range(n_chunks):
        r0 = c * rc
        rows = slice(r0, r0 + rc)

        q_ref[rows, :] = (h_ref[rows, 0:BRANCH_W] * (HEAD_DIM ** -0.5 * LOG2E)).astype(BF16)
        kt_ref[c] = h_ref[rows, BRANCH_W:2 * BRANCH_W].T.astype(BF16)
        v_ref[rows, :] = h_ref[rows, 2 * BRANCH_W:3 * BRANCH_W].astype(BF16)

        u = _gelu_tanh(h_ref[rows, 3 * BRANCH_W:4 * BRANCH_W])
        vv = _gelu_tanh(h_ref[rows, 4 * BRANCH_W:5 * BRANCH_W])
        vn = _layer_norm(vv, sg_g_ref[...], sg_b_ref[...]).astype(BF16)
        r = _dot(w_stack, vn)
        mixed = r[(SG_GROUPS - 1) * rc:SG_GROUPS * rc]
        for g in range(SG_GROUPS - 2, -1, -1):
            mixed = jnp.where(lane_group == g, r[g * rc:(g + 1) * rc], mixed)
        y_ref[rows, 0:BRANCH_W] = (u * (mixed + sgbias_ref[...])).astype(BF16)

        p = h_ref[rows, 5 * BRANCH_W:6 * BRANCH_W]
        pbuf[POOL_HALO + r0:POOL_HALO + r0 + rc, :] = p
        pos1 = (i * ts + r0 + 1 + row_f).astype(F32)
        pooled = []
        for half, (w_lo, w_hi) in enumerate(((POOL_WINDOWS[0], POOL_WINDOWS[1]),
                                             (POOL_WINDOWS[2], POOL_WINDOWS[3]))):
            cols = slice(half * LANES, (half + 1) * LANES)
            s_lo = s_hi = None
            for k in range(SUBLANES):
                starts = [POOL_HALO + r0 - d for d in range(w_hi) if (POOL_HALO + r0 - d) % SUBLANES == k]
                if not starts:
                    continue
                n_rows = max(starts) - min(starts) + rc
                pshift[k, 0:n_rows, :] = pbuf[pl.ds(min(starts), n_rows), cols]
                for st in starts:
                    term = pshift[k, st - min(starts):st - min(starts) + rc, :]
                    if POOL_HALO + r0 - st < w_lo:
                        s_lo = term if s_lo is None else s_lo + term
                    else:
                        s_hi = term if s_hi is None else s_hi + term
            s_hi = s_hi + s_lo
            cnt = jnp.minimum(pos1, jnp.where(low_half, float(w_lo), float(w_hi)))
            pooled.append(jnp.where(low_half, s_lo, s_hi) / cnt - p[:, cols])
        pooled = jnp.concatenate(pooled, axis=1).astype(BF16)
        y_ref[rows, BRANCH_W:2 * BRANCH_W] = (_dot(pooled, poolw_ref[...]) * pscale_ref[...]).astype(BF16)

        a = h_ref[rows, 6 * BRANCH_W:7 * BRANCH_W]
        gl = h_ref[rows, 7 * BRANCH_W:8 * BRANCH_W]
        ybuf[CONV_HALO + r0:CONV_HALO + r0 + rc, :] = a * jax.nn.sigmoid(gl)
        conv = None
        base = CONV_HALO + r0 - (CONV_W - 1)
        for k in range(SUBLANES):
            starts = [base + j for j in range(CONV_W) if (base + j) % SUBLANES == k]
            n_rows = max(starts) - min(starts) + rc
            cshift[k, 0:n_rows, :] = ybuf[pl.ds(min(starts), n_rows), :]
            for st in starts:
                tap = cshift[k, st - min(starts):st - min(starts) + rc, :] * convw_ref[st - base:st - base + 1, :]
                conv = tap if conv is None else conv + tap
        cn = _layer_norm(conv + convb_ref[...], cln_g_ref[...], cln_b_ref[...])
        y_ref[rows, 2 * BRANCH_W:3 * BRANCH_W] = (cn * jax.nn.sigmoid(cn)).astype(BF16)

    pbuf[0:POOL_HALO, :] = pbuf[ts:ts + POOL_HALO, :]
    ybuf[0:CONV_HALO, :] = ybuf[ts:ts + CONV_HALO, :]


def _mix_proj(x, w_in, sg_g, sg_b, sg_w, sg_bias, pool_w, pool_scale, conv_w, conv_b, cln_g, cln_b, *, ts=512):
    b, s, _ = x.shape
    nkb = ts // KEY_BLOCK
    out_shape = (
        jax.ShapeDtypeStruct((b, s, BRANCH_W), BF16),
        jax.ShapeDtypeStruct((b, s // KEY_BLOCK, BRANCH_W, KEY_BLOCK), BF16),
        jax.ShapeDtypeStruct((b, s, BRANCH_W), BF16),
        jax.ShapeDtypeStruct((b, s, 3 * BRANCH_W), BF16),
    )
    return pl.pallas_call(
        functools.partial(_mix_proj_kernel, ts=ts),
        grid=(b, s // ts),
        in_specs=[
            pl.BlockSpec((None, ts, D_MODEL), lambda bi, i: (bi, i, 0)),
            _resident((D_MODEL, IN_COLS)),
            _resident((1, BRANCH_W)),
            _resident((1, BRANCH_W)),
            _resident((SG_GROUPS, SG_CHUNK, SG_CHUNK)),
            _resident((SG_CHUNK, BRANCH_W)),
            _resident((BRANCH_W, BRANCH_W)),
            _resident((1, BRANCH_W)),
            _resident((CONV_HALO, BRANCH_W)),
            _resident((1, BRANCH_W)),
            _resident((1, BRANCH_W)),
            _resident((1, BRANCH_W)),
        ],
        out_specs=(
            pl.BlockSpec((None, ts, BRANCH_W), lambda bi, i: (bi, i, 0)),
            pl.BlockSpec((None, nkb, BRANCH_W, KEY_BLOCK), lambda bi, i: (bi, i, 0, 0)),
            pl.BlockSpec((None, ts, BRANCH_W), lambda bi, i: (bi, i, 0)),
            pl.BlockSpec((None, ts, 3 * BRANCH_W), lambda bi, i: (bi, i, 0)),
        ),
        out_shape=out_shape,
        scratch_shapes=[
            pltpu.VMEM((ts, IN_COLS), F32),
            pltpu.VMEM((ts + POOL_HALO, BRANCH_W), F32),
            pltpu.VMEM((ts + CONV_HALO, BRANCH_W), F32),
            pltpu.VMEM((SUBLANES, SG_CHUNK + POOL_HALO, LANES), F32),
            pltpu.VMEM((SUBLANES, SG_CHUNK + CONV_HALO, BRANCH_W), F32),
        ],
        compiler_params=pltpu.CompilerParams(
            dimension_semantics=("arbitrary", "arbitrary"), vmem_limit_bytes=VMEM_LIMIT_BYTES),
        name="mix_proj",
    )(x, w_in, sg_g, sg_b, sg_w, sg_bias, pool_w, pool_scale, conv_w, conv_b, cln_g, cln_b)


SB_TQ = 256


def _sb_attn_kernel(q_ref, kt_ref, v_ref, u_ref, o_ref,
                    q2_ref, z_ref, lk_ref, tot_ref, w_ref, carry_ref, acc_ref):
    i = pl.program_id(1)
    tq = SB_TQ
    n_pairs = BRANCH_W // LANES
    kb_max = i * (tq // KEY_BLOCK) + (tq // KEY_BLOCK - 1)
    low = lax.broadcasted_iota(jnp.int32, (KEY_BLOCK, LANES), 1) < HEAD_DIM
    low_q = lax.broadcasted_iota(jnp.int32, (tq, LANES), 1) < HEAD_DIM

    for p in range(n_pairs):
        qp = q_ref[:, p * LANES:(p + 1) * LANES]
        zero = jnp.zeros_like(qp)
        q2_ref[p] = jnp.concatenate([jnp.where(low_q, qp, zero), jnp.where(low_q, zero, qp)], axis=0)

    z_ref[1] = jnp.zeros(z_ref.shape[1:], F32)
    lk_ref[1] = jnp.zeros(lk_ref.shape[1:], BF16)
    tot_ref[1] = jnp.zeros(tot_ref.shape[1:], F32)
    w_ref[0] = jnp.zeros(w_ref.shape[1:], BF16)
    carry_ref[...] = jnp.zeros(carry_ref.shape, F32)
    acc_ref[...] = jnp.zeros(acc_ref.shape, F32)

    def causal(kb):
        row = lax.broadcasted_iota(jnp.int32, (2 * tq, KEY_BLOCK), 0)
        col = lax.broadcasted_iota(jnp.int32, (2 * tq, KEY_BLOCK), 1)
        qpos = i * tq + jnp.where(row >= tq, row - tq, row)
        return kb * KEY_BLOCK + col < qpos

    def iteration(t, s, masked):
        o = 1 - s
        kb1 = jnp.maximum(kb_max - t, 0)
        kb2 = kb_max - t + 1
        kb3 = jnp.clip(kb_max - t + 2, 0, kb_max)
        v_rows = pl.ds(pl.multiple_of(kb3 * KEY_BLOCK, KEY_BLOCK), KEY_BLOCK)

        zs, rs, pvs = [], [], []
        for p in range(n_pairs):
            zs.append(_dot(q2_ref[p], kt_ref[kb1, p * LANES:(p + 1) * LANES, :]))
        for p in range(n_pairs):
            rs.append(_dot(lk_ref[o, p], u_ref[...]))
        for p in range(n_pairs):
            vp = v_ref[v_rows, p * LANES:(p + 1) * LANES]
            vzero = jnp.zeros_like(vp)
            v2 = jnp.concatenate([jnp.where(low, vp, vzero), jnp.where(low, vzero, vp)], axis=0)
            pvs.append(_dot(w_ref[s, p], v2))

        for p in range(n_pairs):
            z = zs[p]
            nz = -z
            lk = jnp.minimum(nz, 0.0) - jnp.log(1.0 + jnp.exp2(jnp.minimum(z, nz))) * LOG2E
            if masked:
                lk = jnp.where(causal(kb1), lk, 0.0)
            z_ref[s, p] = z
            lk_ref[s, p] = lk.astype(BF16)
            tot_ref[s, p] = jnp.broadcast_to(jnp.sum(lk, axis=1, keepdims=True), lk.shape)

        for p in range(n_pairs):
            w = jnp.exp2(z_ref[o, p] + rs[p] + carry_ref[p])
            if masked:
                w = jnp.where(causal(kb2), w, 0.0)
            wb = w.astype(BF16)
            w_ref[o, p] = jnp.concatenate([wb[0:tq], wb[tq:2 * tq]], axis=1)
            carry_ref[p] = carry_ref[p] + tot_ref[o, p]

        for p in range(n_pairs):
            acc_ref[p] = acc_ref[p] + pvs[p]

    def make_body(masked, unroll):
        def body(tt, _):
            for k in range(unroll):
                iteration(unroll * tt + k, k % 2, masked)
            return 0
        return body

    make_body(True, 4)(0, 0)
    n_quads = (i + 2) // 2
    lax.fori_loop(1, n_quads, make_body(False, 4), 0)

    @pl.when(i % 2 == 1)
    def _():
        make_body(False, 2)(2 * n_quads, 0)

    for p in range(n_pairs):
        o_ref[:, p * LANES:(p + 1) * LANES] = acc_ref[p].astype(BF16)


def _sb_attn(q, kt, v, u_tri):
    b, s, _ = q.shape
    tq = SB_TQ
    n_pairs = BRANCH_W // LANES
    return pl.pallas_call(
        _sb_attn_kernel,
        grid=(b, s // tq),
        in_specs=[
            pl.BlockSpec((None, tq, BRANCH_W), lambda bi, i: (bi, i, 0)),
            pl.BlockSpec((None, s // KEY_BLOCK, BRANCH_W, KEY_BLOCK), lambda bi, i: (bi, 0, 0, 0)),
            pl.BlockSpec((None, s, BRANCH_W), lambda bi, i: (bi, 0, 0)),
            _resident((KEY_BLOCK, KEY_BLOCK)),
        ],
        out_specs=pl.BlockSpec((None, tq, BRANCH_W), lambda bi, i: (bi, i, 0)),
        out_shape=jax.ShapeDtypeStruct((b, s, BRANCH_W), BF16),
        scratch_shapes=[
            pltpu.VMEM((n_pairs, 2 * tq, LANES), BF16),
            pltpu.VMEM((2, n_pairs, 2 * tq, KEY_BLOCK), F32),
            pltpu.VMEM((2, n_pairs, 2 * tq, KEY_BLOCK), BF16),
            pltpu.VMEM((2, n_pairs, 2 * tq, KEY_BLOCK), F32),
            pltpu.VMEM((2, n_pairs, tq, 2 * KEY_BLOCK), BF16),
            pltpu.VMEM((n_pairs, 2 * tq, KEY_BLOCK), F32),
            pltpu.VMEM((n_pairs, tq, LANES), F32),
        ],
        compiler_params=pltpu.CompilerParams(
            dimension_semantics=("arbitrary", "arbitrary"), vmem_limit_bytes=VMEM_LIMIT_BYTES),
        name="sb_attn",
    )(q, kt, v, u_tri)


def _mix_merge_kernel(x_ref, ya_ref, y_ref, gw_ref, gb_ref, bw_ref, ow_ref, g_ref, b_ref, o_ref, m_ref, *, tn):
    x = x_ref[...]
    xb = x.astype(BF16)
    ys = [ya_ref[...]] + [y_ref[:, k * BRANCH_W:(k + 1) * BRANCH_W] for k in range(3)]
    for c0 in range(0, D_MODEL, tn):
        cols = slice(c0, c0 + tn)
        merged = None
        for n in range(4):
            gate = jax.nn.sigmoid(_dot(xb, gw_ref[n, :, cols]) + gb_ref[n:n + 1, cols])
            term = gate * _dot(ys[n], bw_ref[n, :, cols])
            merged = term if merged is None else merged + term
        m_ref[:, cols] = merged.astype(BF16)
    y = DN_ALPHA * x + _dot(m_ref[...], ow_ref[...])
    o_ref[...] = _layer_norm(y, g_ref[...], b_ref[...])


def _mix_merge(x2, ya, ybcd, gate_w, gate_b, branch_w, out_w, ln_g, ln_b, *, tm=512, tn=256):
    n = x2.shape[0]
    return pl.pallas_call(
        functools.partial(_mix_merge_kernel, tn=tn),
        grid=(n // tm,),
        in_specs=[
            pl.BlockSpec((tm, D_MODEL), lambda i: (i, 0)),
            pl.BlockSpec((tm, BRANCH_W), lambda i: (i, 0)),
            pl.BlockSpec((tm, 3 * BRANCH_W), lambda i: (i, 0)),
            _resident((4, D_MODEL, D_MODEL)),
            _resident((4, D_MODEL)),
            _resident((4, BRANCH_W, D_MODEL)),
            _resident((D_MODEL, D_MODEL)),
            _resident((1, D_MODEL)),
            _resident((1, D_MODEL)),
        ],
        out_specs=pl.BlockSpec((tm, D_MODEL), lambda i: (i, 0)),
        out_shape=jax.ShapeDtypeStruct((n, D_MODEL), F32),
        scratch_shapes=[pltpu.VMEM((tm, D_MODEL), BF16)],
        compiler_params=pltpu.CompilerParams(
            dimension_semantics=("arbitrary",), vmem_limit_bytes=VMEM_LIMIT_BYTES),
        name="mix_merge",
    )(x2, ya, ybcd, gate_w, gate_b, branch_w, out_w, ln_g, ln_b)


def _suffix_sum_matrix():
    j = jnp.arange(KEY_BLOCK)[:, None]
    s = jnp.arange(KEY_BLOCK)[None, :]
    return (j >= s).astype(BF16)


def _block_diag(w):
    g, r, c = w.shape
    out = jnp.zeros((g * r, g * c), w.dtype)
    for k in range(g):
        out = out.at[k * r:(k + 1) * r, k * c:(k + 1) * c].set(w[k])
    return out


def kernel(x, ln_g, ln_b, ffn_w_in, ffn_w_out, mix_w_in, gate_w, gate_b, branch_w, out_w, sg_ln_g, sg_ln_b, sg_w,
           sg_b, pool_w, pool_scale, conv_w, conv_b, conv_ln_g, conv_ln_b):
    b, s, d = x.shape
    n = b * s
    u_ext = _suffix_sum_matrix()
    row = lambda a: a.reshape(1, -1)
    x2 = x.reshape(n, d)
    for l in range(DEPTH):
        x2 = _ffn(x2, row(ln_g[l, 0]), row(ln_b[l, 0]), ffn_w_in[l, 0].astype(BF16), ffn_w_out[l, 0].astype(BF16))
        sg_bias = jnp.repeat(sg_b[l].T, BRANCH_W // SG_GROUPS, axis=1)
        conv_w_pad = jnp.pad(conv_w[l], ((0, CONV_HALO - CONV_W), (0, 0)))
        q, kt, v, ybcd = _mix_proj(
            x2.reshape(b, s, d), mix_w_in[l].astype(BF16), row(sg_ln_g[l]), row(sg_ln_b[l]), sg_w[l], sg_bias,
            _block_diag(pool_w[l]).astype(BF16), row(pool_scale[l]), conv_w_pad, row(conv_b[l]),
            row(conv_ln_g[l]), row(conv_ln_b[l]))
        ya = _sb_attn(q, kt, v, u_ext)
        x2 = _mix_merge(x2, ya.reshape(n, BRANCH_W), ybcd.reshape(n, 3 * BRANCH_W), gate_w[l].astype(BF16),
                        gate_b[l], branch_w[l].astype(BF16), out_w[l].astype(BF16), row(ln_g[l, 1]), row(ln_b[l, 1]))
        x2 = _ffn(x2, row(ln_g[l, 2]), row(ln_b[l, 2]), ffn_w_in[l, 1].astype(BF16), ffn_w_out[l, 1].astype(BF16))
    return x2.reshape(b, s, d)
```

```python
import functools
import math

import jax
import jax.numpy as jnp
from jax import lax
from jax.experimental import pallas as pl
from jax.experimental.pallas import tpu as pltpu

D_MODEL = 1024
DEPTH = 2
BRANCH_W = D_MODEL // 4
HEAD_DIM = 64
SG_CHUNK = 128
SG_GROUPS = 4
POOL_WINDOWS = (2, 4, 8, 16)
CONV_W = 31
D_FF = 2816
IN_COLS = 8 * BRANCH_W
LN_EPS = 1e-5
DN_ALPHA = (2.0 * DEPTH) ** 0.25
LOG2E = 1.0 / math.log(2.0)

LANES = 128
SUBLANES = 8
VMEM_LIMIT_BYTES = 56 * 1024 * 1024

KEY_BLOCK = 128
POOL_HALO = 16
CONV_HALO = 32

F32 = jnp.float32
BF16 = jnp.bfloat16


def _dot(a, b):
    return jnp.dot(a, b, preferred_element_type=F32)


def _layer_norm(y, g, b):
    mu = jnp.mean(y, axis=-1, keepdims=True)
    d = y - mu
    var = jnp.mean(d * d, axis=-1, keepdims=True)
    return d * lax.rsqrt(var + LN_EPS) * g + b


def _resident(shape):
    zeros = (0,) * len(shape)
    return pl.BlockSpec(shape, lambda *_: zeros, pipeline_mode=pl.Buffered(1))


def _ffn_kernel(x_ref, g_ref, b_ref, win_ref, wout_ref, o_ref, *, tf, sub):
    for r0 in range(0, x_ref.shape[0], sub):
        rows = slice(r0, r0 + sub)
        x = x_ref[rows, :]
        xb = x.astype(BF16)
        acc = None
        for c in range(D_FF // tf):
            gate = _dot(xb, win_ref[:, c * tf:(c + 1) * tf])
            up = _dot(xb, win_ref[:, D_FF + c * tf:D_FF + (c + 1) * tf])
            h = (gate * jax.nn.sigmoid(gate) * up).astype(BF16)
            part = _dot(h, wout_ref[c * tf:(c + 1) * tf, :])
            acc = part if acc is None else acc + part
        y = DN_ALPHA * x + 0.5 * acc
        o_ref[rows, :] = _layer_norm(y, g_ref[...], b_ref[...])


def _ffn(x2, ln_g, ln_b, w_in, w_out, *, tm=1024, tf=256, sub=512):
    n = x2.shape[0]
    return pl.pallas_call(
        functools.partial(_ffn_kernel, tf=tf, sub=sub),
        grid=(n // tm,),
        in_specs=[
            pl.BlockSpec((tm, D_MODEL), lambda i: (i, 0)),
            _resident((1, D_MODEL)),
            _resident((1, D_MODEL)),
            _resident((D_MODEL, 2 * D_FF)),
            _resident((D_FF, D_MODEL)),
        ],
        out_specs=pl.BlockSpec((tm, D_MODEL), lambda i: (i, 0)),
        out_shape=jax.ShapeDtypeStruct((n, D_MODEL), F32),
        compiler_params=pltpu.CompilerParams(
            dimension_semantics=("arbitrary",), vmem_limit_bytes=VMEM_LIMIT_BYTES),
        name="ffn",
    )(x2, ln_g, ln_b, w_in, w_out)


def _gelu_tanh(x):
    c = math.sqrt(2.0 / math.pi)
    return x * (0.5 * (1.0 + jnp.tanh(c * (x + 0.044715 * (x * x * x)))))


def _mix_proj_kernel(x_ref, win_ref, sg_g_ref, sg_b_ref, sgw_ref, sgbias_ref, poolw_ref, pscale_ref,
                     convw_ref, convb_ref, cln_g_ref, cln_b_ref,
                     q_ref, kt_ref, v_ref, y_ref,
                     h_ref, pbuf, ybuf, pshift, cshift, *, ts):
    i = pl.program_id(1)
    rc = SG_CHUNK
    n_chunks = ts // rc

    @pl.when(i == 0)
    def _():
        pbuf[0:POOL_HALO, :] = jnp.zeros((POOL_HALO, BRANCH_W), F32)
        ybuf[0:CONV_HALO, :] = jnp.zeros((CONV_HALO, BRANCH_W), F32)

    for r0 in range(0, ts, ts // 2):
        xb = x_ref[r0:r0 + ts // 2, :].astype(BF16)
        for c0 in range(0, IN_COLS, 512):
            h_ref[r0:r0 + ts // 2, c0:c0 + 512] = _dot(xb, win_ref[:, c0:c0 + 512])

    tri = (lax.broadcasted_iota(jnp.int32, (SG_CHUNK, SG_CHUNK), 0)
           >= lax.broadcasted_iota(jnp.int32, (SG_CHUNK, SG_CHUNK), 1))
    w_stack = jnp.concatenate(
        [jnp.where(tri, sgw_ref[g], 0.0).astype(BF16) for g in range(SG_GROUPS)], axis=0)

    lane256 = lax.broadcasted_iota(jnp.int32, (rc, BRANCH_W), 1)
    lane_group = lane256 // (BRANCH_W // SG_GROUPS)
    lane128 = lax.broadcasted_iota(jnp.int32, (rc, LANES), 1)
    low_half = lane128 < (LANES // 2)
    row_f = lax.broadcasted_iota(jnp.int32, (rc, LANES), 0)

    for c in range(n_chunks):
        r0 = c * rc
        rows = slice(r0, r0 + rc)

        q_ref[rows, :] = (h_ref[rows, 0:BRANCH_W] * (HEAD_DIM ** -0.5 * LOG2E)).astype(BF16)
        kt_ref[c] = h_ref[rows, BRANCH_W:2 * BRANCH_W].T.astype(BF16)
        v_ref[rows, :] = h_ref[rows, 2 * BRANCH_W:3 * BRANCH_W].astype(BF16)

        u = _gelu_tanh(h_ref[rows, 3 * BRANCH_W:4 * BRANCH_W])
        vv = _gelu_tanh(h_ref[rows, 4 * BRANCH_W:5 * BRANCH_W])
        vn = _layer_norm(vv, sg_g_ref[...], sg_b_ref[...]).astype(BF16)
        r = _dot(w_stack, vn)
        mixed = r[(SG_GROUPS - 1) * rc:SG_GROUPS * rc]
        for g in range(SG_GROUPS - 2, -1, -1):
            mixed = jnp.where(lane_group == g, r[g * rc:(g + 1) * rc], mixed)
        y_ref[rows, 0:BRANCH_W] = (u * (mixed + sgbias_ref[...])).astype(BF16)

        p = h_ref[rows, 5 * BRANCH_W:6 * BRANCH_W]
        pbuf[POOL_HALO + r0:POOL_HALO + r0 + rc, :] = p
        pos1 = (i * ts + r0 + 1 + row_f).astype(F32)
        pooled = []
        for half, (w_lo, w_hi) in enumerate(((POOL_WINDOWS[0], POOL_WINDOWS[1]),
                                             (POOL_WINDOWS[2], POOL_WINDOWS[3]))):
            cols = slice(half * LANES, (half + 1) * LANES)
            s_lo = s_hi = None
            for k in range(SUBLANES):
                starts = [POOL_HALO + r0 - d for d in range(w_hi) if (POOL_HALO + r0 - d) % SUBLANES == k]
                if not starts:
                    continue
                n_rows = max(starts) - min(starts) + rc
                pshift[k, 0:n_rows, :] = pbuf[pl.ds(min(starts), n_rows), cols]
                for st in starts:
                    term = pshift[k, st - min(starts):st - min(starts) + rc, :]
                    if POOL_HALO + r0 - st < w_lo:
                        s_lo = term if s_lo is None else s_lo + term
                    else:
                        s_hi = term if s_hi is None else s_hi + term
            s_hi = s_hi + s_lo
            cnt = jnp.minimum(pos1, jnp.where(low_half, float(w_lo), float(w_hi)))
            pooled.append(jnp.where(low_half, s_lo, s_hi) / cnt - p[:, cols])
        pooled = jnp.concatenate(pooled, axis=1).astype(BF16)
        y_ref[rows, BRANCH_W:2 * BRANCH_W] = (_dot(pooled, poolw_ref[...]) * pscale_ref[...]).astype(BF16)

        a = h_ref[rows, 6 * BRANCH_W:7 * BRANCH_W]
        gl = h_ref[rows, 7 * BRANCH_W:8 * BRANCH_W]
        ybuf[CONV_HALO + r0:CONV_HALO + r0 + rc, :] = a * jax.nn.sigmoid(gl)
        conv = None
        base = CONV_HALO + r0 - (CONV_W - 1)
        for k in range(SUBLANES):
            starts = [base + j for j in range(CONV_W) if (base + j) % SUBLANES == k]
            n_rows = max(starts) - min(starts) + rc
            cshift[k, 0:n_rows, :] = ybuf[pl.ds(min(starts), n_rows), :]
            for st in starts:
                tap = cshift[k, st - min(starts):st - min(starts) + rc, :] * convw_ref[st - base:st - base + 1, :]
                conv = tap if conv is None else conv + tap
        cn = _layer_norm(conv + convb_ref[...], cln_g_ref[...], cln_b_ref[...])
        y_ref[rows, 2 * BRANCH_W:3 * BRANCH_W] = (cn * jax.nn.sigmoid(cn)).astype(BF16)

    pbuf[0:POOL_HALO, :] = pbuf[ts:ts + POOL_HALO, :]
    ybuf[0:CONV_HALO, :] = ybuf[ts:ts + CONV_HALO, :]


def _mix_proj(x, w_in, sg_g, sg_b, sg_w, sg_bias, pool_w, pool_scale, conv_w, conv_b, cln_g, cln_b, *, ts=512):
    b, s, _ = x.shape
    nkb = ts // KEY_BLOCK
    out_shape = (
        jax.ShapeDtypeStruct((b, s, BRANCH_W), BF16),
        jax.ShapeDtypeStruct((b, s // KEY_BLOCK, BRANCH_W, KEY_BLOCK), BF16),
        jax.ShapeDtypeStruct((b, s, BRANCH_W), BF16),
        jax.ShapeDtypeStruct((b, s, 3 * BRANCH_W), BF16),
    )
    return pl.pallas_call(
        functools.partial(_mix_proj_kernel, ts=ts),
        grid=(b, s // ts),
        in_specs=[
            pl.BlockSpec((None, ts, D_MODEL), lambda bi, i: (bi, i, 0)),
            _resident((D_MODEL, IN_COLS)),
            _resident((1, BRANCH_W)),
            _resident((1, BRANCH_W)),
            _resident((SG_GROUPS, SG_CHUNK, SG_CHUNK)),
            _resident((SG_CHUNK, BRANCH_W)),
            _resident((BRANCH_W, BRANCH_W)),
            _resident((1, BRANCH_W)),
            _resident((CONV_HALO, BRANCH_W)),
            _resident((1, BRANCH_W)),
            _resident((1, BRANCH_W)),
            _resident((1, BRANCH_W)),
        ],
        out_specs=(
            pl.BlockSpec((None, ts, BRANCH_W), lambda bi, i: (bi, i, 0)),
            pl.BlockSpec((None, nkb, BRANCH_W, KEY_BLOCK), lambda bi, i: (bi, i, 0, 0)),
            pl.BlockSpec((None, ts, BRANCH_W), lambda bi, i: (bi, i, 0)),
            pl.BlockSpec((None, ts, 3 * BRANCH_W), lambda bi, i: (bi, i, 0)),
        ),
        out_shape=out_shape,
        scratch_shapes=[
            pltpu.VMEM((ts, IN_COLS), F32),
            pltpu.VMEM((ts + POOL_HALO, BRANCH_W), F32),
            pltpu.VMEM((ts + CONV_HALO, BRANCH_W), F32),
            pltpu.VMEM((SUBLANES, SG_CHUNK + POOL_HALO, LANES), F32),
            pltpu.VMEM((SUBLANES, SG_CHUNK + CONV_HALO, BRANCH_W), F32),
        ],
        compiler_params=pltpu.CompilerParams(
            dimension_semantics=("arbitrary", "arbitrary"), vmem_limit_bytes=VMEM_LIMIT_BYTES),
        name="mix_proj",
    )(x, w_in, sg_g, sg_b, sg_w, sg_bias, pool_w, pool_scale, conv_w, conv_b, cln_g, cln_b)


SB_TQ = 256


def _sb_attn_kernel(q_ref, kt_ref, v_ref, u_ref, o_ref,
                    q2_ref, z_ref, lk_ref, tot_ref, w_ref, carry_ref, acc_ref):
    i = pl.program_id(1)
    tq = SB_TQ
    n_pairs = BRANCH_W // LANES
    kb_max = i * (tq // KEY_BLOCK) + (tq // KEY_BLOCK - 1)
    low = lax.broadcasted_iota(jnp.int32, (KEY_BLOCK, LANES), 1) < HEAD_DIM
    low_q = lax.broadcasted_iota(jnp.int32, (tq, LANES), 1) < HEAD_DIM

    for p in range(n_pairs):
        qp = q_ref[:, p * LANES:(p + 1) * LANES]
        zero = jnp.zeros_like(qp)
        q2_ref[p] = jnp.concatenate([jnp.where(low_q, qp, zero), jnp.where(low_q, zero, qp)], axis=0)

    carry_ref[...] = jnp.zeros(carry_ref.shape, F32)
    acc_ref[...] = jnp.zeros(acc_ref.shape, F32)

    def causal(kb):
        row = lax.broadcasted_iota(jnp.int32, (2 * tq, KEY_BLOCK), 0)
        col = lax.broadcasted_iota(jnp.int32, (2 * tq, KEY_BLOCK), 1)
        qpos = i * tq + jnp.where(row >= tq, row - tq, row)
        return kb * KEY_BLOCK + col < qpos

    def iteration(t, s, masked, s1=True, s2=True, s3=True):
        o = 1 - s
        kb1 = jnp.maximum(kb_max - t, 0)
        kb2 = kb_max - t + 1
        kb3 = jnp.clip(kb_max - t + 2, 0, kb_max)
        v_rows = pl.ds(pl.multiple_of(kb3 * KEY_BLOCK, KEY_BLOCK), KEY_BLOCK)

        zs, rs = [], []
        if s1:
            for p in range(n_pairs):
                zs.append(_dot(q2_ref[p], kt_ref[kb1, p * LANES:(p + 1) * LANES, :]))
        if s2:
            for p in range(n_pairs):
                rs.append(_dot(lk_ref[o, p], u_ref[...]))
        if s3:
            v2s = []
            for p in range(n_pairs):
                vp = v_ref[v_rows, p * LANES:(p + 1) * LANES]
                vzero = jnp.zeros_like(vp)
                v2s.append(jnp.concatenate([jnp.where(low, vp, vzero), jnp.where(low, vzero, vp)], axis=0))
            vz = jnp.zeros_like(v2s[0])
            rhs = jnp.concatenate(
                [jnp.concatenate([v2s[0], vz], axis=1), jnp.concatenate([vz, v2s[1]], axis=1)], axis=0)
            pv = _dot(jnp.concatenate([w_ref[s, 0], w_ref[s, 1]], axis=1), rhs)

        if s1:
            for p in range(n_pairs):
                z = zs[p]
                nz = -z
                lk = jnp.minimum(nz, 0.0) - jnp.log(1.0 + jnp.exp2(jnp.minimum(z, nz))) * LOG2E
                if masked:
                    lk = jnp.where(causal(kb1), lk, 0.0)
                z_ref[s, p] = z
                lk_ref[s, p] = lk.astype(BF16)
                tot_ref[s, p] = jnp.broadcast_to(jnp.sum(lk, axis=1, keepdims=True), lk.shape)

        if s2:
            for p in range(n_pairs):
                w = jnp.exp2(z_ref[o, p] + rs[p] + carry_ref[p])
                if masked:
                    w = jnp.where(causal(kb2), w, 0.0)
                wb = w.astype(BF16)
                w_ref[o, p] = jnp.concatenate([wb[0:tq], wb[tq:2 * tq]], axis=1)
                carry_ref[p] = carry_ref[p] + tot_ref[o, p]

        if s3:
            acc_ref[...] = acc_ref[...] + pv

    def make_body(unroll):
        def body(tt, _):
            for k in range(unroll):
                iteration(unroll * tt + k, k % 2, False)
            return 0
        return body

    iteration(0, 0, True, s2=False, s3=False)
    iteration(1, 1, True, s3=False)
    iteration(2, 0, True)
    iteration(3, 1, True)

    n_quads = jnp.maximum(i - 1, 0) // 2
    lax.fori_loop(1, 1 + n_quads, make_body(4), 0)

    @pl.when((i >= 2) & (i % 2 == 0))
    def _():
        make_body(2)(2 + 2 * n_quads, 0)

    @pl.when(i >= 1)
    def _():
        iteration(2 * i + 2, 0, False, s1=False)
        iteration(2 * i + 3, 1, False, s1=False, s2=False)

    o_ref[...] = acc_ref[...].astype(BF16)


def _sb_attn(q, kt, v, u_tri):
    b, s, _ = q.shape
    tq = SB_TQ
    n_pairs = BRANCH_W // LANES
    return pl.pallas_call(
        _sb_attn_kernel,
        grid=(b, s // tq),
        in_specs=[
            pl.BlockSpec((None, tq, BRANCH_W), lambda bi, i: (bi, i, 0)),
            pl.BlockSpec((None, s // KEY_BLOCK, BRANCH_W, KEY_BLOCK), lambda bi, i: (bi, 0, 0, 0)),
            pl.BlockSpec((None, s, BRANCH_W), lambda bi, i: (bi, 0, 0)),
            _resident((KEY_BLOCK, KEY_BLOCK)),
        ],
        out_specs=pl.BlockSpec((None, tq, BRANCH_W), lambda bi, i: (bi, i, 0)),
        out_shape=jax.ShapeDtypeStruct((b, s, BRANCH_W), BF16),
        scratch_shapes=[
            pltpu.VMEM((n_pairs, 2 * tq, LANES), BF16),
            pltpu.VMEM((2, n_pairs, 2 * tq, KEY_BLOCK), F32),
            pltpu.VMEM((2, n_pairs, 2 * tq, KEY_BLOCK), BF16),
            pltpu.VMEM((2, n_pairs, 2 * tq, KEY_BLOCK), F32),
            pltpu.VMEM((2, n_pairs, tq, 2 * KEY_BLOCK), BF16),
            pltpu.VMEM((n_pairs, 2 * tq, KEY_BLOCK), F32),
            pltpu.VMEM((tq, BRANCH_W), F32),
        ],
        compiler_params=pltpu.CompilerParams(
            dimension_semantics=("arbitrary", "arbitrary"), vmem_limit_bytes=VMEM_LIMIT_BYTES),
        name="sb_attn",
    )(q, kt, v, u_tri)


def _mix_merge_kernel(x_ref, ya_ref, y_ref, gw_ref, gb_ref, bw_ref, ow_ref, g_ref, b_ref, o_ref, m_ref, *, tn, sub):
    for r0 in range(0, x_ref.shape[0], sub):
        rows = slice(r0, r0 + sub)
        x = x_ref[rows, :]
        xb = x.astype(BF16)
        ys = [ya_ref[rows, :]] + [y_ref[rows, k * BRANCH_W:(k + 1) * BRANCH_W] for k in range(3)]
        for c0 in range(0, D_MODEL, tn):
            cols = slice(c0, c0 + tn)
            merged = None
            for n in range(4):
                gate = jax.nn.sigmoid(_dot(xb, gw_ref[n, :, cols]) + gb_ref[n:n + 1, cols])
                term = gate * _dot(ys[n], bw_ref[n, :, cols])
                merged = term if merged is None else merged + term
            m_ref[rows, cols] = merged.astype(BF16)
        y = DN_ALPHA * x + _dot(m_ref[rows, :], ow_ref[...])
        o_ref[rows, :] = _layer_norm(y, g_ref[...], b_ref[...])


def _mix_merge(x2, ya, ybcd, gate_w, gate_b, branch_w, out_w, ln_g, ln_b, *, tm=1024, tn=256, sub=512):
    n = x2.shape[0]
    return pl.pallas_call(
        functools.partial(_mix_merge_kernel, tn=tn, sub=sub),
        grid=(n // tm,),
        in_specs=[
            pl.BlockSpec((tm, D_MODEL), lambda i: (i, 0)),
            pl.BlockSpec((tm, BRANCH_W), lambda i: (i, 0)),
            pl.BlockSpec((tm, 3 * BRANCH_W), lambda i: (i, 0)),
            _resident((4, D_MODEL, D_MODEL)),
            _resident((4, D_MODEL)),
            _resident((4, BRANCH_W, D_MODEL)),
            _resident((D_MODEL, D_MODEL)),
            _resident((1, D_MODEL)),
            _resident((1, D_MODEL)),
        ],
        out_specs=pl.BlockSpec((tm, D_MODEL), lambda i: (i, 0)),
        out_shape=jax.ShapeDtypeStruct((n, D_MODEL), F32),
        scratch_shapes=[pltpu.VMEM((tm, D_MODEL), BF16)],
        compiler_params=pltpu.CompilerParams(
            dimension_semantics=("arbitrary",), vmem_limit_bytes=VMEM_LIMIT_BYTES),
        name="mix_merge",
    )(x2, ya, ybcd, gate_w, gate_b, branch_w, out_w, ln_g, ln_b)


def _suffix_sum_matrix():
    j = jnp.arange(KEY_BLOCK)[:, None]
    s = jnp.arange(KEY_BLOCK)[None, :]
    return (j >= s).astype(BF16)


def _block_diag(w):
    g, r, c = w.shape
    out = jnp.zeros((g * r, g * c), w.dtype)
    for k in range(g):
        out = out.at[k * r:(k + 1) * r, k * c:(k + 1) * c].set(w[k])
    return out


def kernel(x, ln_g, ln_b, ffn_w_in, ffn_w_out, mix_w_in, gate_w, gate_b, branch_w, out_w, sg_ln_g, sg_ln_b, sg_w,
           sg_b, pool_w, pool_scale, conv_w, conv_b, conv_ln_g, conv_ln_b):
    b, s, d = x.shape
    n = b * s
    u_ext = _suffix_sum_matrix()
    row = lambda a: a.reshape(1, -1)
    x2 = x.reshape(n, d)
    for l in range(DEPTH):
        x2 = _ffn(x2, row(ln_g[l, 0]), row(ln_b[l, 0]), ffn_w_in[l, 0].astype(BF16), ffn_w_out[l, 0].astype(BF16))
        sg_bias = jnp.repeat(sg_b[l].T, BRANCH_W // SG_GROUPS, axis=1)
        conv_w_pad = jnp.pad(conv_w[l], ((0, CONV_HALO - CONV_W), (0, 0)))
        q, kt, v, ybcd = _mix_proj(
            x2.reshape(b, s, d), mix_w_in[l].astype(BF16), row(sg_ln_g[l]), row(sg_ln_b[l]), sg_w[l], sg_bias,
            _block_diag(pool_w[l]).astype(BF16), row(pool_scale[l]), conv_w_pad, row(conv_b[l]),
            row(conv_ln_g[l]), row(conv_ln_b[l]))
        ya = _sb_attn(q, kt, v, u_ext)
        x2 = _mix_merge(x2, ya.reshape(n, BRANCH_W), ybcd.reshape(n, 3 * BRANCH_W), gate_w[l].astype(BF16),
                        gate_b[l], branch_w[l].astype(BF16), out_w[l].astype(BF16), row(ln_g[l, 1]), row(ln_b[l, 1]))
        x2 = _ffn(x2, row(ln_g[l, 2]), row(ln_b[l, 2]), ffn_w_in[l, 1].astype(BF16), ffn_w_out[l, 1].astype(BF16))
    return x2.reshape(b, s, d)
```

```python
import functools
import math

import jax
import jax.numpy as jnp
from jax import lax
from jax.experimental import pallas as pl
from jax.experimental.pallas import tpu as pltpu

D_MODEL = 1024
DEPTH = 2
BRANCH_W = D_MODEL // 4
HEAD_DIM = 64
SG_CHUNK = 128
SG_GROUPS = 4
POOL_WINDOWS = (2, 4, 8, 16)
CONV_W = 31
D_FF = 2816
IN_COLS = 8 * BRANCH_W
LN_EPS = 1e-5
DN_ALPHA = (2.0 * DEPTH) ** 0.25
LOG2E = 1.0 / math.log(2.0)

LANES = 128
SUBLANES = 8
VMEM_LIMIT_BYTES = 56 * 1024 * 1024

KEY_BLOCK = 128
POOL_HALO = 16
CONV_HALO = 32

F32 = jnp.float32
BF16 = jnp.bfloat16


def _dot(a, b):
    return jnp.dot(a, b, preferred_element_type=F32)


def _layer_norm(y, g, b):
    mu = jnp.mean(y, axis=-1, keepdims=True)
    d = y - mu
    var = jnp.mean(d * d, axis=-1, keepdims=True)
    return d * lax.rsqrt(var + LN_EPS) * g + b


def _resident(shape):
    zeros = (0,) * len(shape)
    return pl.BlockSpec(shape, lambda *_: zeros, pipeline_mode=pl.Buffered(1))


def _ffn_kernel(x_ref, g_ref, b_ref, win_ref, wout_ref, o_ref, *, tf, sub):
    for r0 in range(0, x_ref.shape[0], sub):
        rows = slice(r0, r0 + sub)
        x = x_ref[rows, :]
        xb = x.astype(BF16)
        acc = None
        for c in range(D_FF // tf):
            gate = _dot(xb, win_ref[:, c * tf:(c + 1) * tf])
            up = _dot(xb, win_ref[:, D_FF + c * tf:D_FF + (c + 1) * tf])
            h = (gate * jax.nn.sigmoid(gate) * up).astype(BF16)
            part = _dot(h, wout_ref[c * tf:(c + 1) * tf, :])
            acc = part if acc is None else acc + part
        y = DN_ALPHA * x + 0.5 * acc
        o_ref[rows, :] = _layer_norm(y, g_ref[...], b_ref[...])


def _ffn(x2, ln_g, ln_b, w_in, w_out, *, tm=1024, tf=256, sub=512):
    n = x2.shape[0]
    return pl.pallas_call(
        functools.partial(_ffn_kernel, tf=tf, sub=sub),
        grid=(n // tm,),
        in_specs=[
            pl.BlockSpec((tm, D_MODEL), lambda i: (i, 0)),
            _resident((1, D_MODEL)),
            _resident((1, D_MODEL)),
            _resident((D_MODEL, 2 * D_FF)),
            _resident((D_FF, D_MODEL)),
        ],
        out_specs=pl.BlockSpec((tm, D_MODEL), lambda i: (i, 0)),
        out_shape=jax.ShapeDtypeStruct((n, D_MODEL), F32),
        compiler_params=pltpu.CompilerParams(
            dimension_semantics=("arbitrary",), vmem_limit_bytes=VMEM_LIMIT_BYTES),
        name="ffn",
    )(x2, ln_g, ln_b, w_in, w_out)


def _gelu_tanh(x):
    c = math.sqrt(2.0 / math.pi)
    return x * (0.5 * (1.0 + jnp.tanh(c * (x + 0.044715 * (x * x * x)))))


def _mix_proj_kernel(x_ref, win_ref, sg_g_ref, sg_b_ref, sgw_ref, sgbias_ref, poolw_ref, pscale_ref,
                     convw_ref, convb_ref, cln_g_ref, cln_b_ref,
                     q_ref, kt_ref, v_ref, y_ref,
                     h_ref, pbuf, ybuf, pshift, cshift, *, ts):
    i = pl.program_id(1)
    rc = SG_CHUNK
    n_chunks = ts // rc

    @pl.when(i == 0)
    def _():
        pbuf[0:POOL_HALO, :] = jnp.zeros((POOL_HALO, BRANCH_W), F32)
        ybuf[0:CONV_HALO, :] = jnp.zeros((CONV_HALO, BRANCH_W), F32)

    for r0 in range(0, ts, ts // 2):
        xb = x_ref[r0:r0 + ts // 2, :].astype(BF16)
        for c0 in range(0, IN_COLS, 512):
            h_ref[r0:r0 + ts // 2, c0:c0 + 512] = _dot(xb, win_ref[:, c0:c0 + 512])

    tri = (lax.broadcasted_iota(jnp.int32, (SG_CHUNK, SG_CHUNK), 0)
           >= lax.broadcasted_iota(jnp.int32, (SG_CHUNK, SG_CHUNK), 1))
    w_stack = jnp.concatenate(
        [jnp.where(tri, sgw_ref[g], 0.0).astype(BF16) for g in range(SG_GROUPS)], axis=0)

    lane256 = lax.broadcasted_iota(jnp.int32, (rc, BRANCH_W), 1)
    lane_group = lane256 // (BRANCH_W // SG_GROUPS)
    lane128 = lax.broadcasted_iota(jnp.int32, (rc, LANES), 1)
    low_half = lane128 < (LANES // 2)
    row_f = lax.broadcasted_iota(jnp.int32, (rc, LANES), 0)

    for c in range(n_chunks):
        r0 = c * rc
        rows = slice(r0, r0 + rc)

        q_ref[rows, :] = (h_ref[rows, 0:BRANCH_W] * (HEAD_DIM ** -0.5 * LOG2E)).astype(BF16)
        kt_ref[c] = h_ref[rows, BRANCH_W:2 * BRANCH_W].T.astype(BF16)
        v_ref[rows, :] = h_ref[rows, 2 * BRANCH_W:3 * BRANCH_W].astype(BF16)

        u = _gelu_tanh(h_ref[rows, 3 * BRANCH_W:4 * BRANCH_W])
        vv = _gelu_tanh(h_ref[rows, 4 * BRANCH_W:5 * BRANCH_W])
        vn = _layer_norm(vv, sg_g_ref[...], sg_b_ref[...]).astype(BF16)
        r = _dot(w_stack, vn)
        mixed = r[(SG_GROUPS - 1) * rc:SG_GROUPS * rc]
        for g in range(SG_GROUPS - 2, -1, -1):
            mixed = jnp.where(lane_group == g, r[g * rc:(g + 1) * rc], mixed)
        y_ref[rows, 0:BRANCH_W] = (u * (mixed + sgbias_ref[...])).astype(BF16)

        p = h_ref[rows, 5 * BRANCH_W:6 * BRANCH_W]
        pbuf[POOL_HALO + r0:POOL_HALO + r0 + rc, :] = p
        pos1 = (i * ts + r0 + 1 + row_f).astype(F32)
        pooled = []
        for half, (w_lo, w_hi) in enumerate(((POOL_WINDOWS[0], POOL_WINDOWS[1]),
                                             (POOL_WINDOWS[2], POOL_WINDOWS[3]))):
            cols = slice(half * LANES, (half + 1) * LANES)
            s_lo = s_hi = None
            for k in range(SUBLANES):
                starts = [POOL_HALO + r0 - d for d in range(w_hi) if (POOL_HALO + r0 - d) % SUBLANES == k]
                if not starts:
                    continue
                n_rows = max(starts) - min(starts) + rc
                pshift[k, 0:n_rows, :] = pbuf[pl.ds(min(starts), n_rows), cols]
                for st in starts:
                    term = pshift[k, st - min(starts):st - min(starts) + rc, :]
                    if POOL_HALO + r0 - st < w_lo:
                        s_lo = term if s_lo is None else s_lo + term
                    else:
                        s_hi = term if s_hi is None else s_hi + term
            s_hi = s_hi + s_lo
            cnt = jnp.minimum(pos1, jnp.where(low_half, float(w_lo), float(w_hi)))
            pooled.append(jnp.where(low_half, s_lo, s_hi) / cnt - p[:, cols])
        pooled = jnp.concatenate(pooled, axis=1).astype(BF16)
        y_ref[rows, BRANCH_W:2 * BRANCH_W] = (_dot(pooled, poolw_ref[...]) * pscale_ref[...]).astype(BF16)

        a = h_ref[rows, 6 * BRANCH_W:7 * BRANCH_W]
        gl = h_ref[rows, 7 * BRANCH_W:8 * BRANCH_W]
        ybuf[CONV_HALO + r0:CONV_HALO + r0 + rc, :] = a * jax.nn.sigmoid(gl)
        conv = None
        base = CONV_HALO + r0 - (CONV_W - 1)
        for k in range(SUBLANES):
            starts = [base + j for j in range(CONV_W) if (base + j) % SUBLANES == k]
            n_rows = max(starts) - min(starts) + rc
            cshift[k, 0:n_rows, :] = ybuf[pl.ds(min(starts), n_rows), :]
            for st in starts:
                tap = cshift[k, st - min(starts):st - min(starts) + rc, :] * convw_ref[st - base:st - base + 1, :]
                conv = tap if conv is None else conv + tap
        cn = _layer_norm(conv + convb_ref[...], cln_g_ref[...], cln_b_ref[...])
        y_ref[rows, 2 * BRANCH_W:3 * BRANCH_W] = (cn * jax.nn.sigmoid(cn)).astype(BF16)

    pbuf[0:POOL_HALO, :] = pbuf[ts:ts + POOL_HALO, :]
    ybuf[0:CONV_HALO, :] = ybuf[ts:ts + CONV_HALO, :]


def _mix_proj(x, w_in, sg_g, sg_b, sg_w, sg_bias, pool_w, pool_scale, conv_w, conv_b, cln_g, cln_b, *, ts=512):
    b, s, _ = x.shape
    nkb = ts // KEY_BLOCK
    out_shape = (
        jax.ShapeDtypeStruct((b, s, BRANCH_W), BF16),
        jax.ShapeDtypeStruct((b, s // KEY_BLOCK, BRANCH_W, KEY_BLOCK), BF16),
        jax.ShapeDtypeStruct((b, s, BRANCH_W), BF16),
        jax.ShapeDtypeStruct((b, s, 3 * BRANCH_W), BF16),
    )
    return pl.pallas_call(
        functools.partial(_mix_proj_kernel, ts=ts),
        grid=(b, s // ts),
        in_specs=[
            pl.BlockSpec((None, ts, D_MODEL), lambda bi, i: (bi, i, 0)),
            _resident((D_MODEL, IN_COLS)),
            _resident((1, BRANCH_W)),
            _resident((1, BRANCH_W)),
            _resident((SG_GROUPS, SG_CHUNK, SG_CHUNK)),
            _resident((SG_CHUNK, BRANCH_W)),
            _resident((BRANCH_W, BRANCH_W)),
            _resident((1, BRANCH_W)),
            _resident((CONV_HALO, BRANCH_W)),
            _resident((1, BRANCH_W)),
            _resident((1, BRANCH_W)),
            _resident((1, BRANCH_W)),
        ],
        out_specs=(
            pl.BlockSpec((None, ts, BRANCH_W), lambda bi, i: (bi, i, 0)),
            pl.BlockSpec((None, nkb, BRANCH_W, KEY_BLOCK), lambda bi, i: (bi, i, 0, 0)),
            pl.BlockSpec((None, ts, BRANCH_W), lambda bi, i: (bi, i, 0)),
            pl.BlockSpec((None, ts, 3 * BRANCH_W), lambda bi, i: (bi, i, 0)),
        ),
        out_shape=out_shape,
        scratch_shapes=[
            pltpu.VMEM((ts, IN_COLS), F32),
            pltpu.VMEM((ts + POOL_HALO, BRANCH_W), F32),
            pltpu.VMEM((ts + CONV_HALO, BRANCH_W), F32),
            pltpu.VMEM((SUBLANES, SG_CHUNK + POOL_HALO, LANES), F32),
            pltpu.VMEM((SUBLANES, SG_CHUNK + CONV_HALO, BRANCH_W), F32),
        ],
        compiler_params=pltpu.CompilerParams(
            dimension_semantics=("arbitrary", "arbitrary"), vmem_limit_bytes=VMEM_LIMIT_BYTES),
        name="mix_proj",
    )(x, w_in, sg_g, sg_b, sg_w, sg_bias, pool_w, pool_scale, conv_w, conv_b, cln_g, cln_b)


SB_TQ = 512


def _sb_attn_kernel(q_ref, kt_ref, v_ref, u_ref, o_ref,
                    q2_ref, z_ref, lk_ref, tot_ref, w_ref, carry_ref, acc_ref):
    i = pl.program_id(1)
    tq = SB_TQ
    n_pairs = BRANCH_W // LANES
    kb_max = i * (tq // KEY_BLOCK) + (tq // KEY_BLOCK - 1)
    low = lax.broadcasted_iota(jnp.int32, (KEY_BLOCK, LANES), 1) < HEAD_DIM
    low_q = lax.broadcasted_iota(jnp.int32, (tq, LANES), 1) < HEAD_DIM

    for p in range(n_pairs):
        qp = q_ref[:, p * LANES:(p + 1) * LANES]
        zero = jnp.zeros_like(qp)
        q2_ref[p] = jnp.concatenate([jnp.where(low_q, qp, zero), jnp.where(low_q, zero, qp)], axis=0)

    carry_ref[...] = jnp.zeros(carry_ref.shape, F32)
    acc_ref[...] = jnp.zeros(acc_ref.shape, F32)

    def causal(kb):
        row = lax.broadcasted_iota(jnp.int32, (2 * tq, KEY_BLOCK), 0)
        col = lax.broadcasted_iota(jnp.int32, (2 * tq, KEY_BLOCK), 1)
        qpos = i * tq + jnp.where(row >= tq, row - tq, row)
        return kb * KEY_BLOCK + col < qpos

    def iteration(t, s, masked, s1=True, s2=True, s3=True):
        o = 1 - s
        kb1 = jnp.maximum(kb_max - t, 0)
        kb2 = kb_max - t + 1
        kb3 = jnp.clip(kb_max - t + 2, 0, kb_max)
        v_rows = pl.ds(pl.multiple_of(kb3 * KEY_BLOCK, KEY_BLOCK), KEY_BLOCK)

        zs, rs = [], []
        if s1:
            for p in range(n_pairs):
                zs.append(_dot(q2_ref[p], kt_ref[kb1, p * LANES:(p + 1) * LANES, :]))
        if s2:
            for p in range(n_pairs):
                rs.append(_dot(lk_ref[o, p], u_ref[...]))
        if s3:
            v2s = []
            for p in range(n_pairs):
                vp = v_ref[v_rows, p * LANES:(p + 1) * LANES]
                vzero = jnp.zeros_like(vp)
                v2s.append(jnp.concatenate([jnp.where(low, vp, vzero), jnp.where(low, vzero, vp)], axis=0))
            vz = jnp.zeros_like(v2s[0])
            rhs = jnp.concatenate(
                [jnp.concatenate([v2s[0], vz], axis=1), jnp.concatenate([vz, v2s[1]], axis=1)], axis=0)
            pv = _dot(jnp.concatenate([w_ref[s, 0], w_ref[s, 1]], axis=1), rhs)

        if s1:
            for p in range(n_pairs):
                z = zs[p]
                nz = -z
                lk = jnp.minimum(nz, 0.0) - jnp.log(1.0 + jnp.exp2(jnp.minimum(z, nz))) * LOG2E
                if masked:
                    lk = jnp.where(causal(kb1), lk, 0.0)
                z_ref[s, p] = z
                lk_ref[s, p] = lk.astype(BF16)
                tot_ref[s, p] = jnp.broadcast_to(jnp.sum(lk, axis=1, keepdims=True), lk.shape)

        if s2:
            for p in range(n_pairs):
                w = jnp.exp2(z_ref[o, p] + rs[p] + carry_ref[p])
                if masked:
                    w = jnp.where(causal(kb2), w, 0.0)
                wb = w.astype(BF16)
                w_ref[o, p] = jnp.concatenate([wb[0:tq], wb[tq:2 * tq]], axis=1)
                carry_ref[p] = carry_ref[p] + tot_ref[o, p]

        if s3:
            acc_ref[...] = acc_ref[...] + pv

    n_diag = tq // KEY_BLOCK
    n_items = n_diag * (i + 1)
    head = n_diag + 2
    iteration(0, 0, True, s2=False, s3=False)
    iteration(1, 1, True, s3=False)
    for t in range(2, head):
        iteration(t, t % 2, True)

    n_mid = jnp.maximum(n_items - head, 0)
    n_quads = n_mid // 4

    def middle(t0, count):
        for k in range(count):
            iteration(t0 + k, k % 2, False)

    def quad(tt, carry):
        middle(head + 4 * tt, 4)
        return carry

    lax.fori_loop(0, n_quads, quad, 0)

    @pl.when(n_mid % 4 == 2)
    def _():
        middle(head + 4 * n_quads, 2)

    @pl.when(i >= 1)
    def _():
        iteration(n_items, 0, False, s1=False)
        iteration(n_items + 1, 1, False, s1=False, s2=False)

    o_ref[...] = acc_ref[...].astype(BF16)


def _sb_attn(q, kt, v, u_tri):
    b, s, _ = q.shape
    tq = SB_TQ
    n_pairs = BRANCH_W // LANES
    return pl.pallas_call(
        _sb_attn_kernel,
        grid=(b, s // tq),
        in_specs=[
            pl.BlockSpec((None, tq, BRANCH_W), lambda bi, i: (bi, i, 0)),
            pl.BlockSpec((None, s // KEY_BLOCK, BRANCH_W, KEY_BLOCK), lambda bi, i: (bi, 0, 0, 0)),
            pl.BlockSpec((None, s, BRANCH_W), lambda bi, i: (bi, 0, 0)),
            _resident((KEY_BLOCK, KEY_BLOCK)),
        ],
        out_specs=pl.BlockSpec((None, tq, BRANCH_W), lambda bi, i: (bi, i, 0)),
        out_shape=jax.ShapeDtypeStruct((b, s, BRANCH_W), BF16),
        scratch_shapes=[
            pltpu.VMEM((n_pairs, 2 * tq, LANES), BF16),
            pltpu.VMEM((2, n_pairs, 2 * tq, KEY_BLOCK), F32),
            pltpu.VMEM((2, n_pairs, 2 * tq, KEY_BLOCK), BF16),
            pltpu.VMEM((2, n_pairs, 2 * tq, KEY_BLOCK), F32),
            pltpu.VMEM((2, n_pairs, tq, 2 * KEY_BLOCK), BF16),
            pltpu.VMEM((n_pairs, 2 * tq, KEY_BLOCK), F32),
            pltpu.VMEM((tq, BRANCH_W), F32),
        ],
        compiler_params=pltpu.CompilerParams(
            dimension_semantics=("arbitrary", "arbitrary"), vmem_limit_bytes=VMEM_LIMIT_BYTES),
        name="sb_attn",
    )(q, kt, v, u_tri)


def _mix_merge_kernel(x_ref, ya_ref, y_ref, gw_ref, gb_ref, bw_ref, ow_ref, g_ref, b_ref, o_ref, m_ref, *, tn, sub):
    for r0 in range(0, x_ref.shape[0], sub):
        rows = slice(r0, r0 + sub)
        x = x_ref[rows, :]
        xb = x.astype(BF16)
        ys = [ya_ref[rows, :]] + [y_ref[rows, k * BRANCH_W:(k + 1) * BRANCH_W] for k in range(3)]
        for c0 in range(0, D_MODEL, tn):
            cols = slice(c0, c0 + tn)
            merged = None
            for n in range(4):
                gate = jax.nn.sigmoid(_dot(xb, gw_ref[n, :, cols]) + gb_ref[n:n + 1, cols])
                term = gate * _dot(ys[n], bw_ref[n, :, cols])
                merged = term if merged is None else merged + term
            m_ref[rows, cols] = merged.astype(BF16)
        y = DN_ALPHA * x + _dot(m_ref[rows, :], ow_ref[...])
        o_ref[rows, :] = _layer_norm(y, g_ref[...], b_ref[...])


def _mix_merge(x2, ya, ybcd, gate_w, gate_b, branch_w, out_w, ln_g, ln_b, *, tm=1024, tn=256, sub=512):
    n = x2.shape[0]
    return pl.pallas_call(
        functools.partial(_mix_merge_kernel, tn=tn, sub=sub),
        grid=(n // tm,),
        in_specs=[
            pl.BlockSpec((tm, D_MODEL), lambda i: (i, 0)),
            pl.BlockSpec((tm, BRANCH_W), lambda i: (i, 0)),
            pl.BlockSpec((tm, 3 * BRANCH_W), lambda i: (i, 0)),
            _resident((4, D_MODEL, D_MODEL)),
            _resident((4, D_MODEL)),
            _resident((4, BRANCH_W, D_MODEL)),
            _resident((D_MODEL, D_MODEL)),
            _resident((1, D_MODEL)),
            _resident((1, D_MODEL)),
        ],
        out_specs=pl.BlockSpec((tm, D_MODEL), lambda i: (i, 0)),
        out_shape=jax.ShapeDtypeStruct((n, D_MODEL), F32),
        scratch_shapes=[pltpu.VMEM((tm, D_MODEL), BF16)],
        compiler_params=pltpu.CompilerParams(
            dimension_semantics=("arbitrary",), vmem_limit_bytes=VMEM_LIMIT_BYTES),
        name="mix_merge",
    )(x2, ya, ybcd, gate_w, gate_b, branch_w, out_w, ln_g, ln_b)


def _suffix_sum_matrix():
    j = jnp.arange(KEY_BLOCK)[:, None]
    s = jnp.arange(KEY_BLOCK)[None, :]
    return (j >= s).astype(BF16)


def _block_diag(w):
    g, r, c = w.shape
    out = jnp.zeros((g * r, g * c), w.dtype)
    for k in range(g):
        out = out.at[k * r:(k + 1) * r, k * c:(k + 1) * c].set(w[k])
    return out


def kernel(x, ln_g, ln_b, ffn_w_in, ffn_w_out, mix_w_in, gate_w, gate_b, branch_w, out_w, sg_ln_g, sg_ln_b, sg_w,
           sg_b, pool_w, pool_scale, conv_w, conv_b, conv_ln_g, conv_ln_b):
    b, s, d = x.shape
    n = b * s
    u_ext = _suffix_sum_matrix()
    row = lambda a: a.reshape(1, -1)
    x2 = x.reshape(n, d)
    for l in range(DEPTH):
        x2 = _ffn(x2, row(ln_g[l, 0]), row(ln_b[l, 0]), ffn_w_in[l, 0].astype(BF16), ffn_w_out[l, 0].astype(BF16))
        sg_bias = jnp.repeat(sg_b[l].T, BRANCH_W // SG_GROUPS, axis=1)
        conv_w_pad = jnp.pad(conv_w[l], ((0, CONV_HALO - CONV_W), (0, 0)))
        q, kt, v, ybcd = _mix_proj(
            x2.reshape(b, s, d), mix_w_in[l].astype(BF16), row(sg_ln_g[l]), row(sg_ln_b[l]), sg_w[l], sg_bias,
            _block_diag(pool_w[l]).astype(BF16), row(pool_scale[l]), conv_w_pad, row(conv_b[l]),
            row(conv_ln_g[l]), row(conv_ln_b[l]))
        ya = _sb_attn(q, kt, v, u_ext)
        x2 = _mix_merge(x2, ya.reshape(n, BRANCH_W), ybcd.reshape(n, 3 * BRANCH_W), gate_w[l].astype(BF16),
                        gate_b[l], branch_w[l].astype(BF16), out_w[l].astype(BF16), row(ln_g[l, 1]), row(ln_b[l, 1]))
        x2 = _ffn(x2, row(ln_g[l, 2]), row(ln_b[l, 2]), ffn_w_in[l, 1].astype(BF16), ffn_w_out[l, 1].astype(BF16))
    return x2.reshape(b, s, d)
```

```python
import functools
import math

import jax
import jax.numpy as jnp
from jax import lax
from jax.experimental import pallas as pl
from jax.experimental.pallas import tpu as pltpu

D_MODEL = 1024
DEPTH = 2
BRANCH_W = D_MODEL // 4
HEAD_DIM = 64
SG_CHUNK = 128
SG_GROUPS = 4
POOL_WINDOWS = (2, 4, 8, 16)
CONV_W = 31
D_FF = 2816
IN_COLS = 8 * BRANCH_W
LN_EPS = 1e-5
DN_ALPHA = (2.0 * DEPTH) ** 0.25
LOG2E = 1.0 / math.log(2.0)

LANES = 128
SUBLANES = 8
VMEM_LIMIT_BYTES = 56 * 1024 * 1024

KEY_BLOCK = 128
POOL_HALO = 16
CONV_HALO = 32

F32 = jnp.float32
BF16 = jnp.bfloat16


def _dot(a, b):
    return jnp.dot(a, b, preferred_element_type=F32)


def _layer_norm(y, g, b):
    mu = jnp.mean(y, axis=-1, keepdims=True)
    d = y - mu
    var = jnp.mean(d * d, axis=-1, keepdims=True)
    return d * lax.rsqrt(var + LN_EPS) * g + b


def _resident(shape):
    zeros = (0,) * len(shape)
    return pl.BlockSpec(shape, lambda *_: zeros, pipeline_mode=pl.Buffered(1))


def _ffn_kernel(x_ref, g_ref, b_ref, win_ref, wout_ref, o_ref, *, tf, sub):
    for r0 in range(0, x_ref.shape[0], sub):
        rows = slice(r0, r0 + sub)
        x = x_ref[rows, :]
        xb = x.astype(BF16)
        acc = None
        for c in range(D_FF // tf):
            gate = _dot(xb, win_ref[:, c * tf:(c + 1) * tf])
            up = _dot(xb, win_ref[:, D_FF + c * tf:D_FF + (c + 1) * tf])
            h = (gate * jax.nn.sigmoid(gate) * up).astype(BF16)
            part = _dot(h, wout_ref[c * tf:(c + 1) * tf, :])
            acc = part if acc is None else acc + part
        y = DN_ALPHA * x + 0.5 * acc
        o_ref[rows, :] = _layer_norm(y, g_ref[...], b_ref[...])


def _ffn(x2, ln_g, ln_b, w_in, w_out, *, tm=1024, tf=256, sub=512):
    n = x2.shape[0]
    return pl.pallas_call(
        functools.partial(_ffn_kernel, tf=tf, sub=sub),
        grid=(n // tm,),
        in_specs=[
            pl.BlockSpec((tm, D_MODEL), lambda i: (i, 0)),
            _resident((1, D_MODEL)),
            _resident((1, D_MODEL)),
            _resident((D_MODEL, 2 * D_FF)),
            _resident((D_FF, D_MODEL)),
        ],
        out_specs=pl.BlockSpec((tm, D_MODEL), lambda i: (i, 0)),
        out_shape=jax.ShapeDtypeStruct((n, D_MODEL), F32),
        compiler_params=pltpu.CompilerParams(
            dimension_semantics=("arbitrary",), vmem_limit_bytes=VMEM_LIMIT_BYTES),
        name="ffn",
    )(x2, ln_g, ln_b, w_in, w_out)


def _gelu_tanh(x):
    c = math.sqrt(2.0 / math.pi)
    return x * (0.5 * (1.0 + jnp.tanh(c * (x + 0.044715 * (x * x * x)))))


def _mix_proj_kernel(x_ref, win_ref, sg_g_ref, sg_b_ref, sgw_ref, sgbias_ref, poolw_ref, pscale_ref,
                     convw_ref, convb_ref, cln_g_ref, cln_b_ref,
                     q_ref, kt_ref, v_ref, y_ref,
                     h_ref, pbuf, ybuf, pshift, cshift, *, ts):
    i = pl.program_id(1)
    rc = SG_CHUNK
    n_chunks = ts // rc

    @pl.when(i == 0)
    def _():
        pbuf[0:POOL_HALO, :] = jnp.zeros((POOL_HALO, BRANCH_W), F32)
        ybuf[0:CONV_HALO, :] = jnp.zeros((CONV_HALO, BRANCH_W), F32)

    for r0 in range(0, ts, ts // 2):
        xb = x_ref[r0:r0 + ts // 2, :].astype(BF16)
        for c0 in range(0, IN_COLS, 512):
            h_ref[r0:r0 + ts // 2, c0:c0 + 512] = _dot(xb, win_ref[:, c0:c0 + 512])

    tri = (lax.broadcasted_iota(jnp.int32, (SG_CHUNK, SG_CHUNK), 0)
           >= lax.broadcasted_iota(jnp.int32, (SG_CHUNK, SG_CHUNK), 1))
    w_stack = jnp.concatenate(
        [jnp.where(tri, sgw_ref[g], 0.0).astype(BF16) for g in range(SG_GROUPS)], axis=0)

    lane256 = lax.broadcasted_iota(jnp.int32, (rc, BRANCH_W), 1)
    lane_group = lane256 // (BRANCH_W // SG_GROUPS)
    lane128 = lax.broadcasted_iota(jnp.int32, (rc, LANES), 1)
    low_half = lane128 < (LANES // 2)
    row_f = lax.broadcasted_iota(jnp.int32, (rc, LANES), 0)

    for c in range(n_chunks):
        r0 = c * rc
        rows = slice(r0, r0 + rc)

        q_ref[rows, :] = (h_ref[rows, 0:BRANCH_W] * (HEAD_DIM ** -0.5 * LOG2E)).astype(BF16)
        kt_ref[c] = h_ref[rows, BRANCH_W:2 * BRANCH_W].T.astype(BF16)
        v_ref[rows, :] = h_ref[rows, 2 * BRANCH_W:3 * BRANCH_W].astype(BF16)

        u = _gelu_tanh(h_ref[rows, 3 * BRANCH_W:4 * BRANCH_W])
        vv = _gelu_tanh(h_ref[rows, 4 * BRANCH_W:5 * BRANCH_W])
        vn = _layer_norm(vv, sg_g_ref[...], sg_b_ref[...]).astype(BF16)
        r = _dot(w_stack, vn)
        mixed = r[(SG_GROUPS - 1) * rc:SG_GROUPS * rc]
        for g in range(SG_GROUPS - 2, -1, -1):
            mixed = jnp.where(lane_group == g, r[g * rc:(g + 1) * rc], mixed)
        y_ref[rows, 0:BRANCH_W] = (u * (mixed + sgbias_ref[...])).astype(BF16)

        p = h_ref[rows, 5 * BRANCH_W:6 * BRANCH_W]
        pbuf[POOL_HALO + r0:POOL_HALO + r0 + rc, :] = p
        pos1 = (i * ts + r0 + 1 + row_f).astype(F32)
        pooled = []
        for half, (w_lo, w_hi) in enumerate(((POOL_WINDOWS[0], POOL_WINDOWS[1]),
                                             (POOL_WINDOWS[2], POOL_WINDOWS[3]))):
            cols = slice(half * LANES, (half + 1) * LANES)
            s_lo = s_hi = None
            for k in range(SUBLANES):
                starts = [POOL_HALO + r0 - d for d in range(w_hi) if (POOL_HALO + r0 - d) % SUBLANES == k]
                if not starts:
                    continue
                n_rows = max(starts) - min(starts) + rc
                pshift[k, 0:n_rows, :] = pbuf[pl.ds(min(starts), n_rows), cols]
                for st in starts:
                    term = pshift[k, st - min(starts):st - min(starts) + rc, :]
                    if POOL_HALO + r0 - st < w_lo:
                        s_lo = term if s_lo is None else s_lo + term
                    else:
                        s_hi = term if s_hi is None else s_hi + term
            s_hi = s_hi + s_lo
            cnt = jnp.minimum(pos1, jnp.where(low_half, float(w_lo), float(w_hi)))
            pooled.append(jnp.where(low_half, s_lo, s_hi) / cnt - p[:, cols])
        pooled = jnp.concatenate(pooled, axis=1).astype(BF16)
        y_ref[rows, BRANCH_W:2 * BRANCH_W] = (_dot(pooled, poolw_ref[...]) * pscale_ref[...]).astype(BF16)

        a = h_ref[rows, 6 * BRANCH_W:7 * BRANCH_W]
        gl = h_ref[rows, 7 * BRANCH_W:8 * BRANCH_W]
        ybuf[CONV_HALO + r0:CONV_HALO + r0 + rc, :] = a * jax.nn.sigmoid(gl)
        conv = None
        base = CONV_HALO + r0 - (CONV_W - 1)
        for k in range(SUBLANES):
            starts = [base + j for j in range(CONV_W) if (base + j) % SUBLANES == k]
            n_rows = max(starts) - min(starts) + rc
            cshift[k, 0:n_rows, :] = ybuf[pl.ds(min(starts), n_rows), :]
            for st in starts:
                tap = cshift[k, st - min(starts):st - min(starts) + rc, :] * convw_ref[st - base:st - base + 1, :]
                conv = tap if conv is None else conv + tap
        cn = _layer_norm(conv + convb_ref[...], cln_g_ref[...], cln_b_ref[...])
        y_ref[rows, 2 * BRANCH_W:3 * BRANCH_W] = (cn * jax.nn.sigmoid(cn)).astype(BF16)

    pbuf[0:POOL_HALO, :] = pbuf[ts:ts + POOL_HALO, :]
    ybuf[0:CONV_HALO, :] = ybuf[ts:ts + CONV_HALO, :]


def _mix_proj(x, w_in, sg_g, sg_b, sg_w, sg_bias, pool_w, pool_scale, conv_w, conv_b, cln_g, cln_b, *, ts=512):
    b, s, _ = x.shape
    nkb = ts // KEY_BLOCK
    out_shape = (
        jax.ShapeDtypeStruct((b, s, BRANCH_W), BF16),
        jax.ShapeDtypeStruct((b, s // KEY_BLOCK, BRANCH_W, KEY_BLOCK), BF16),
        jax.ShapeDtypeStruct((b, s, BRANCH_W), BF16),
        jax.ShapeDtypeStruct((b, s, 3 * BRANCH_W), BF16),
    )
    return pl.pallas_call(
        functools.partial(_mix_proj_kernel, ts=ts),
        grid=(b, s // ts),
        in_specs=[
            pl.BlockSpec((None, ts, D_MODEL), lambda bi, i: (bi, i, 0)),
            _resident((D_MODEL, IN_COLS)),
            _resident((1, BRANCH_W)),
            _resident((1, BRANCH_W)),
            _resident((SG_GROUPS, SG_CHUNK, SG_CHUNK)),
            _resident((SG_CHUNK, BRANCH_W)),
            _resident((BRANCH_W, BRANCH_W)),
            _resident((1, BRANCH_W)),
            _resident((CONV_HALO, BRANCH_W)),
            _resident((1, BRANCH_W)),
            _resident((1, BRANCH_W)),
            _resident((1, BRANCH_W)),
        ],
        out_specs=(
            pl.BlockSpec((None, ts, BRANCH_W), lambda bi, i: (bi, i, 0)),
            pl.BlockSpec((None, nkb, BRANCH_W, KEY_BLOCK), lambda bi, i: (bi, i, 0, 0)),
            pl.BlockSpec((None, ts, BRANCH_W), lambda bi, i: (bi, i, 0)),
            pl.BlockSpec((None, ts, 3 * BRANCH_W), lambda bi, i: (bi, i, 0)),
        ),
        out_shape=out_shape,
        scratch_shapes=[
            pltpu.VMEM((ts, IN_COLS), F32),
            pltpu.VMEM((ts + POOL_HALO, BRANCH_W), F32),
            pltpu.VMEM((ts + CONV_HALO, BRANCH_W), F32),
            pltpu.VMEM((SUBLANES, SG_CHUNK + POOL_HALO, LANES), F32),
            pltpu.VMEM((SUBLANES, SG_CHUNK + CONV_HALO, BRANCH_W), F32),
        ],
        compiler_params=pltpu.CompilerParams(
            dimension_semantics=("arbitrary", "arbitrary"), vmem_limit_bytes=VMEM_LIMIT_BYTES),
        name="mix_proj",
    )(x, w_in, sg_g, sg_b, sg_w, sg_bias, pool_w, pool_scale, conv_w, conv_b, cln_g, cln_b)


SB_TQ = 512


def _sb_attn_kernel(q_ref, kt_ref, v_ref, u_ref, o_ref,
                    q2_ref, dq_ref, z_ref, lk_ref, tot_ref, w_ref, carry_ref, acc_ref):
    i = pl.program_id(1)
    tq = SB_TQ
    n_pairs = BRANCH_W // LANES
    kb_max = i * (tq // KEY_BLOCK) + (tq // KEY_BLOCK - 1)
    low = lax.broadcasted_iota(jnp.int32, (KEY_BLOCK, LANES), 1) < HEAD_DIM
    low_q = lax.broadcasted_iota(jnp.int32, (tq, LANES), 1) < HEAD_DIM

    for p in range(n_pairs):
        qp = q_ref[:, p * LANES:(p + 1) * LANES]
        zero = jnp.zeros_like(qp)
        q2_ref[p] = jnp.concatenate([jnp.where(low_q, qp, zero), jnp.where(low_q, zero, qp)], axis=0)

    carry_ref[...] = jnp.zeros(carry_ref.shape, F32)
    acc_ref[...] = jnp.zeros(acc_ref.shape, F32)

    row = lax.broadcasted_iota(jnp.int32, (2 * tq, KEY_BLOCK), 0)
    col = lax.broadcasted_iota(jnp.int32, (2 * tq, KEY_BLOCK), 1)
    dq_ref[...] = i * tq + jnp.where(row >= tq, row - tq, row) - col

    def stacked_rows(ref_rows, a):
        return jnp.concatenate([ref_rows(a, tq), ref_rows(tq + a, 2 * tq)], axis=0)

    def iteration(t, s, mask1=False, mask2=False, skip=0, s1=True, s2=True, s3=True):
        a = skip * KEY_BLOCK
        nh = tq - a
        o = 1 - s
        kb1 = jnp.maximum(kb_max - t, 0)
        kb2 = kb_max - t + 1
        kb3 = jnp.clip(kb_max - t + 2, 0, kb_max)
        v_rows = pl.ds(pl.multiple_of(kb3 * KEY_BLOCK, KEY_BLOCK), KEY_BLOCK)

        zs, rs = [], []
        if s1:
            for p in range(n_pairs):
                lhs = q2_ref[p] if a == 0 else stacked_rows(lambda r0, r1, p=p: q2_ref[p, r0:r1, :], a)
                zs.append(_dot(lhs, kt_ref[kb1, p * LANES:(p + 1) * LANES, :]))
        if s2:
            for p in range(n_pairs):
                rs.append(_dot(lk_ref[o, p], u_ref[...]))
        if s3:
            v2s = []
            for p in range(n_pairs):
                vp = v_ref[v_rows, p * LANES:(p + 1) * LANES]
                vzero = jnp.zeros_like(vp)
                v2s.append(jnp.concatenate([jnp.where(low, vp, vzero), jnp.where(low, vzero, vp)], axis=0))
            vz = jnp.zeros_like(v2s[0])
            rhs = jnp.concatenate(
                [jnp.concatenate([v2s[0], vz], axis=1), jnp.concatenate([vz, v2s[1]], axis=1)], axis=0)
            pv = _dot(jnp.concatenate([w_ref[s, 0], w_ref[s, 1]], axis=1), rhs)

        if s1:
            for p in range(n_pairs):
                z = zs[p]
                nz = -z
                lk = jnp.minimum(nz, 0.0) - jnp.log(1.0 + jnp.exp2(jnp.minimum(z, nz))) * LOG2E
                if mask1:
                    dq = dq_ref[...] if a == 0 else stacked_rows(lambda r0, r1: dq_ref[r0:r1, :], a)
                    lk = jnp.where(kb1 * KEY_BLOCK < dq, lk, 0.0)
                lkb = lk.astype(BF16)
                tot = jnp.broadcast_to(jnp.sum(lk, axis=1, keepdims=True), lk.shape)
                for hh in range(2):
                    dst = slice(hh * tq + a, (hh + 1) * tq)
                    src = slice(hh * nh, (hh + 1) * nh)
                    z_ref[s, p, dst, :] = z[src]
                    lk_ref[s, p, dst, :] = lkb[src]
                    tot_ref[s, p, dst, :] = tot[src]
                    if a:
                        pad = slice(hh * tq, hh * tq + a)
                        z_ref[s, p, pad, :] = jnp.zeros((a, KEY_BLOCK), F32)
                        lk_ref[s, p, pad, :] = jnp.zeros((a, KEY_BLOCK), BF16)
                        tot_ref[s, p, pad, :] = jnp.zeros((a, KEY_BLOCK), F32)

        if s2:
            for p in range(n_pairs):
                w = jnp.exp2(z_ref[o, p] + rs[p] + carry_ref[p])
                if mask2:
                    w = jnp.where(kb2 * KEY_BLOCK < dq_ref[...], w, 0.0)
                wb = w.astype(BF16)
                w_ref[o, p] = jnp.concatenate([wb[0:tq], wb[tq:2 * tq]], axis=1)
                carry_ref[p] = carry_ref[p] + tot_ref[o, p]

        if s3:
            acc_ref[...] = acc_ref[...] + pv

    n_diag = tq // KEY_BLOCK
    n_items = n_diag * (i + 1)
    head = n_diag + 2
    for t in range(head):
        iteration(t, t % 2, mask1=t < n_diag, mask2=t - 1 < n_diag, skip=max(n_diag - 1 - t, 0),
                  s2=t >= 1, s3=t >= 2)

    n_mid = jnp.maximum(n_items - head, 0)
    n_quads = n_mid // 4

    def middle(t0, count):
        for k in range(count):
            iteration(t0 + k, k % 2)

    def quad(tt, carry):
        middle(head + 4 * tt, 4)
        return carry

    lax.fori_loop(0, n_quads, quad, 0)

    @pl.when(n_mid % 4 == 2)
    def _():
        middle(head + 4 * n_quads, 2)

    @pl.when(i >= 1)
    def _():
        iteration(n_items, 0, s1=False)
        iteration(n_items + 1, 1, s1=False, s2=False)

    o_ref[...] = acc_ref[...].astype(BF16)


def _sb_attn(q, kt, v, u_tri):
    b, s, _ = q.shape
    tq = SB_TQ
    n_pairs = BRANCH_W // LANES
    return pl.pallas_call(
        _sb_attn_kernel,
        grid=(b, s // tq),
        in_specs=[
            pl.BlockSpec((None, tq, BRANCH_W), lambda bi, i: (bi, i, 0)),
            pl.BlockSpec((None, s // KEY_BLOCK, BRANCH_W, KEY_BLOCK), lambda bi, i: (bi, 0, 0, 0)),
            pl.BlockSpec((None, s, BRANCH_W), lambda bi, i: (bi, 0, 0)),
            _resident((KEY_BLOCK, KEY_BLOCK)),
        ],
        out_specs=pl.BlockSpec((None, tq, BRANCH_W), lambda bi, i: (bi, i, 0)),
        out_shape=jax.ShapeDtypeStruct((b, s, BRANCH_W), BF16),
        scratch_shapes=[
            pltpu.VMEM((n_pairs, 2 * tq, LANES), BF16),
            pltpu.VMEM((2 * tq, KEY_BLOCK), jnp.int32),
            pltpu.VMEM((2, n_pairs, 2 * tq, KEY_BLOCK), F32),
            pltpu.VMEM((2, n_pairs, 2 * tq, KEY_BLOCK), BF16),
            pltpu.VMEM((2, n_pairs, 2 * tq, KEY_BLOCK), F32),
            pltpu.VMEM((2, n_pairs, tq, 2 * KEY_BLOCK), BF16),
            pltpu.VMEM((n_pairs, 2 * tq, KEY_BLOCK), F32),
            pltpu.VMEM((tq, BRANCH_W), F32),
        ],
        compiler_params=pltpu.CompilerParams(
            dimension_semantics=("arbitrary", "arbitrary"), vmem_limit_bytes=VMEM_LIMIT_BYTES),
        name="sb_attn",
    )(q, kt, v, u_tri)


def _mix_merge_kernel(x_ref, ya_ref, y_ref, gw_ref, gb_ref, bw_ref, ow_ref, g_ref, b_ref, o_ref, m_ref, *, tn, sub):
    for r0 in range(0, x_ref.shape[0], sub):
        rows = slice(r0, r0 + sub)
        x = x_ref[rows, :]
        xb = x.astype(BF16)
        ys = [ya_ref[rows, :]] + [y_ref[rows, k * BRANCH_W:(k + 1) * BRANCH_W] for k in range(3)]
        for c0 in range(0, D_MODEL, tn):
            cols = slice(c0, c0 + tn)
            merged = None
            for n in range(4):
                gate = jax.nn.sigmoid(_dot(xb, gw_ref[n, :, cols]) + gb_ref[n:n + 1, cols])
                term = gate * _dot(ys[n], bw_ref[n, :, cols])
                merged = term if merged is None else merged + term
            m_ref[rows, cols] = merged.astype(BF16)
        y = DN_ALPHA * x + _dot(m_ref[rows, :], ow_ref[...])
        o_ref[rows, :] = _layer_norm(y, g_ref[...], b_ref[...])


def _mix_merge(x2, ya, ybcd, gate_w, gate_b, branch_w, out_w, ln_g, ln_b, *, tm=1024, tn=256, sub=512):
    n = x2.shape[0]
    return pl.pallas_call(
        functools.partial(_mix_merge_kernel, tn=tn, sub=sub),
        grid=(n // tm,),
        in_specs=[
            pl.BlockSpec((tm, D_MODEL), lambda i: (i, 0)),
            pl.BlockSpec((tm, BRANCH_W), lambda i: (i, 0)),
            pl.BlockSpec((tm, 3 * BRANCH_W), lambda i: (i, 0)),
            _resident((4, D_MODEL, D_MODEL)),
            _resident((4, D_MODEL)),
            _resident((4, BRANCH_W, D_MODEL)),
            _resident((D_MODEL, D_MODEL)),
            _resident((1, D_MODEL)),
            _resident((1, D_MODEL)),
        ],
        out_specs=pl.BlockSpec((tm, D_MODEL), lambda i: (i, 0)),
        out_shape=jax.ShapeDtypeStruct((n, D_MODEL), F32),
        scratch_shapes=[pltpu.VMEM((tm, D_MODEL), BF16)],
        compiler_params=pltpu.CompilerParams(
            dimension_semantics=("arbitrary",), vmem_limit_bytes=VMEM_LIMIT_BYTES),
        name="mix_merge",
    )(x2, ya, ybcd, gate_w, gate_b, branch_w, out_w, ln_g, ln_b)


def _suffix_sum_matrix():
    j = jnp.arange(KEY_BLOCK)[:, None]
    s = jnp.arange(KEY_BLOCK)[None, :]
    return (j >= s).astype(BF16)


def _block_diag(w):
    g, r, c = w.shape
    out = jnp.zeros((g * r, g * c), w.dtype)
    for k in range(g):
        out = out.at[k * r:(k + 1) * r, k * c:(k + 1) * c].set(w[k])
    return out


def kernel(x, ln_g, ln_b, ffn_w_in, ffn_w_out, mix_w_in, gate_w, gate_b, branch_w, out_w, sg_ln_g, sg_ln_b, sg_w,
           sg_b, pool_w, pool_scale, conv_w, conv_b, conv_ln_g, conv_ln_b):
    b, s, d = x.shape
    n = b * s
    u_ext = _suffix_sum_matrix()
    row = lambda a: a.reshape(1, -1)
    x2 = x.reshape(n, d)
    for l in range(DEPTH):
        x2 = _ffn(x2, row(ln_g[l, 0]), row(ln_b[l, 0]), ffn_w_in[l, 0].astype(BF16), ffn_w_out[l, 0].astype(BF16))
        sg_bias = jnp.repeat(sg_b[l].T, BRANCH_W // SG_GROUPS, axis=1)
        conv_w_pad = jnp.pad(conv_w[l], ((0, CONV_HALO - CONV_W), (0, 0)))
        q, kt, v, ybcd = _mix_proj(
            x2.reshape(b, s, d), mix_w_in[l].astype(BF16), row(sg_ln_g[l]), row(sg_ln_b[l]), sg_w[l], sg_bias,
            _block_diag(pool_w[l]).astype(BF16), row(pool_scale[l]), conv_w_pad, row(conv_b[l]),
            row(conv_ln_g[l]), row(conv_ln_b[l]))
        ya = _sb_attn(q, kt, v, u_ext)
        x2 = _mix_merge(x2, ya.reshape(n, BRANCH_W), ybcd.reshape(n, 3 * BRANCH_W), gate_w[l].astype(BF16),
                        gate_b[l], branch_w[l].astype(BF16), out_w[l].astype(BF16), row(ln_g[l, 1]), row(ln_b[l, 1]))
        x2 = _ffn(x2, row(ln_g[l, 2]), row(ln_b[l, 2]), ffn_w_in[l, 1].astype(BF16), ffn_w_out[l, 1].astype(BF16))
    return x2.reshape(b, s, d)
```

```python
import functools
import math

import jax
import jax.numpy as jnp
from jax import lax
from jax.experimental import pallas as pl
from jax.experimental.pallas import tpu as pltpu

D_MODEL = 1024
DEPTH = 2
BRANCH_W = D_MODEL // 4
HEAD_DIM = 64
SG_CHUNK = 128
SG_GROUPS = 4
POOL_WINDOWS = (2, 4, 8, 16)
CONV_W = 31
D_FF = 2816
IN_COLS = 8 * BRANCH_W
LN_EPS = 1e-5
DN_ALPHA = (2.0 * DEPTH) ** 0.25
LOG2E = 1.0 / math.log(2.0)

LANES = 128
SUBLANES = 8
VMEM_LIMIT_BYTES = 56 * 1024 * 1024

KEY_BLOCK = 128
POOL_HALO = 16
CONV_HALO = 32
PROJ_COLS = 512

F32 = jnp.float32
BF16 = jnp.bfloat16


def _dot(a, b):
    return jnp.dot(a, b, preferred_element_type=F32)


def _layer_norm(y, g, b):
    mu = jnp.mean(y, axis=-1, keepdims=True)
    d = y - mu
    var = jnp.mean(d * d, axis=-1, keepdims=True)
    return d * lax.rsqrt(var + LN_EPS) * g + b


def _resident(shape):
    zeros = (0,) * len(shape)
    return pl.BlockSpec(shape, lambda *_: zeros, pipeline_mode=pl.Buffered(1))


def _ffn_kernel(x_ref, g_ref, b_ref, win_ref, wout_ref, o_ref, *, tf, sub):
    for r0 in range(0, x_ref.shape[0], sub):
        rows = slice(r0, r0 + sub)
        x = x_ref[rows, :]
        xb = x.astype(BF16)
        acc = None
        for c in range(D_FF // tf):
            gate = _dot(xb, win_ref[:, c * tf:(c + 1) * tf])
            up = _dot(xb, win_ref[:, D_FF + c * tf:D_FF + (c + 1) * tf])
            h = (gate * jax.nn.sigmoid(gate) * up).astype(BF16)
            part = _dot(h, wout_ref[c * tf:(c + 1) * tf, :])
            acc = part if acc is None else acc + part
        y = DN_ALPHA * x + 0.5 * acc
        o_ref[rows, :] = _layer_norm(y, g_ref[...], b_ref[...])


def _ffn(x2, ln_g, ln_b, w_in, w_out, *, tm=1024, tf=256, sub=512):
    n = x2.shape[0]
    return pl.pallas_call(
        functools.partial(_ffn_kernel, tf=tf, sub=sub),
        grid=(n // tm,),
        in_specs=[
            pl.BlockSpec((tm, D_MODEL), lambda i: (i, 0)),
            _resident((1, D_MODEL)),
            _resident((1, D_MODEL)),
            _resident((D_MODEL, 2 * D_FF)),
            _resident((D_FF, D_MODEL)),
        ],
        out_specs=pl.BlockSpec((tm, D_MODEL), lambda i: (i, 0)),
        out_shape=jax.ShapeDtypeStruct((n, D_MODEL), F32),
        compiler_params=pltpu.CompilerParams(
            dimension_semantics=("arbitrary",), vmem_limit_bytes=VMEM_LIMIT_BYTES),
        name="ffn",
    )(x2, ln_g, ln_b, w_in, w_out)


def _gelu_tanh(x):
    c = math.sqrt(2.0 / math.pi)
    return x * (0.5 * (1.0 + jnp.tanh(c * (x + 0.044715 * (x * x * x)))))


def _mix_proj_kernel(x_ref, win_ref, sg_g_ref, sg_b_ref, sgw_ref, sgbias_ref, poolw_ref, pscale_ref,
                     convw_ref, convb_ref, cln_g_ref, cln_b_ref,
                     q_ref, kt_ref, v_ref, y_ref,
                     h_ref, pbuf, ybuf, pshift, cshift, *, ts):
    i = pl.program_id(1)
    rc = SG_CHUNK
    n_chunks = ts // rc

    @pl.when(i == 0)
    def _():
        pbuf[0:POOL_HALO, :] = jnp.zeros((POOL_HALO, BRANCH_W), F32)
        ybuf[0:CONV_HALO, :] = jnp.zeros((CONV_HALO, BRANCH_W), F32)

    for r0 in range(0, ts, ts // 2):
        xb = x_ref[r0:r0 + ts // 2, :].astype(BF16)
        for c0 in range(0, IN_COLS, PROJ_COLS):
            h_ref[r0:r0 + ts // 2, c0:c0 + PROJ_COLS] = _dot(xb, win_ref[:, c0:c0 + PROJ_COLS])

    tri = (lax.broadcasted_iota(jnp.int32, (SG_CHUNK, SG_CHUNK), 0)
           >= lax.broadcasted_iota(jnp.int32, (SG_CHUNK, SG_CHUNK), 1))
    w_groups = [jnp.where(tri, sgw_ref[g], 0.0).astype(BF16) for g in range(SG_GROUPS)]

    lane256 = lax.broadcasted_iota(jnp.int32, (rc, BRANCH_W), 1)
    lane_group = lane256 // (BRANCH_W // SG_GROUPS)
    lane128 = lax.broadcasted_iota(jnp.int32, (rc, LANES), 1)
    low_half = lane128 < (LANES // 2)
    row_f = lax.broadcasted_iota(jnp.int32, (rc, LANES), 0)

    for c in range(n_chunks):
        r0 = c * rc
        rows = slice(r0, r0 + rc)

        q_ref[rows, :] = (h_ref[rows, 0:BRANCH_W] * (HEAD_DIM ** -0.5 * LOG2E)).astype(BF16)
        kt_ref[c] = h_ref[rows, BRANCH_W:2 * BRANCH_W].T.astype(BF16)
        v_ref[rows, :] = h_ref[rows, 2 * BRANCH_W:3 * BRANCH_W].astype(BF16)

        u = _gelu_tanh(h_ref[rows, 3 * BRANCH_W:4 * BRANCH_W])
        vv = _gelu_tanh(h_ref[rows, 4 * BRANCH_W:5 * BRANCH_W])
        vn = _layer_norm(vv, sg_g_ref[...], sg_b_ref[...]).astype(BF16)
        mixed = None
        for g in range(SG_GROUPS):
            r = _dot(w_groups[g], vn)
            mixed = r if mixed is None else jnp.where(lane_group == g, r, mixed)
        y_ref[rows, 0:BRANCH_W] = (u * (mixed + sgbias_ref[...])).astype(BF16)

        p = h_ref[rows, 5 * BRANCH_W:6 * BRANCH_W]
        pbuf[POOL_HALO + r0:POOL_HALO + r0 + rc, :] = p
        pos1 = (i * ts + r0 + 1 + row_f).astype(F32)
        pooled = []
        for half, (w_lo, w_hi) in enumerate(((POOL_WINDOWS[0], POOL_WINDOWS[1]),
                                             (POOL_WINDOWS[2], POOL_WINDOWS[3]))):
            cols = slice(half * LANES, (half + 1) * LANES)
            s_lo = s_hi = None
            for k in range(SUBLANES):
                starts = [POOL_HALO + r0 - d for d in range(w_hi) if (POOL_HALO + r0 - d) % SUBLANES == k]
                if not starts:
                    continue
                n_rows = max(starts) - min(starts) + rc
                pshift[k, 0:n_rows, :] = pbuf[pl.ds(min(starts), n_rows), cols]
                for st in starts:
                    term = pshift[k, st - min(starts):st - min(starts) + rc, :]
                    if POOL_HALO + r0 - st < w_lo:
                        s_lo = term if s_lo is None else s_lo + term
                    else:
                        s_hi = term if s_hi is None else s_hi + term
            s_hi = s_hi + s_lo
            cnt = jnp.minimum(pos1, jnp.where(low_half, float(w_lo), float(w_hi)))
            pooled.append(jnp.where(low_half, s_lo, s_hi) / cnt - p[:, cols])
        pooled = jnp.concatenate(pooled, axis=1).astype(BF16)
        y_ref[rows, BRANCH_W:2 * BRANCH_W] = (_dot(pooled, poolw_ref[...]) * pscale_ref[...]).astype(BF16)

        a = h_ref[rows, 6 * BRANCH_W:7 * BRANCH_W]
        gl = h_ref[rows, 7 * BRANCH_W:8 * BRANCH_W]
        ybuf[CONV_HALO + r0:CONV_HALO + r0 + rc, :] = a * jax.nn.sigmoid(gl)
        conv = None
        base = CONV_HALO + r0 - (CONV_W - 1)
        for k in range(SUBLANES):
            starts = [base + j for j in range(CONV_W) if (base + j) % SUBLANES == k]
            n_rows = max(starts) - min(starts) + rc
            cshift[k, 0:n_rows, :] = ybuf[pl.ds(min(starts), n_rows), :]
            for st in starts:
                tap = cshift[k, st - min(starts):st - min(starts) + rc, :] * convw_ref[st - base:st - base + 1, :]
                conv = tap if conv is None else conv + tap
        cn = _layer_norm(conv + convb_ref[...], cln_g_ref[...], cln_b_ref[...])
        y_ref[rows, 2 * BRANCH_W:3 * BRANCH_W] = (cn * jax.nn.sigmoid(cn)).astype(BF16)

    pbuf[0:POOL_HALO, :] = pbuf[ts:ts + POOL_HALO, :]
    ybuf[0:CONV_HALO, :] = ybuf[ts:ts + CONV_HALO, :]


def _mix_proj(x, w_in, sg_g, sg_b, sg_w, sg_bias, pool_w, pool_scale, conv_w, conv_b, cln_g, cln_b, *, ts=512):
    b, s, _ = x.shape
    nkb = ts // KEY_BLOCK
    out_shape = (
        jax.ShapeDtypeStruct((b, s, BRANCH_W), BF16),
        jax.ShapeDtypeStruct((b, s // KEY_BLOCK, BRANCH_W, KEY_BLOCK), BF16),
        jax.ShapeDtypeStruct((b, s, BRANCH_W), BF16),
        jax.ShapeDtypeStruct((b, s, 3 * BRANCH_W), BF16),
    )
    return pl.pallas_call(
        functools.partial(_mix_proj_kernel, ts=ts),
        grid=(b, s // ts),
        in_specs=[
            pl.BlockSpec((None, ts, D_MODEL), lambda bi, i: (bi, i, 0)),
            _resident((D_MODEL, IN_COLS)),
            _resident((1, BRANCH_W)),
            _resident((1, BRANCH_W)),
            _resident((SG_GROUPS, SG_CHUNK, SG_CHUNK)),
            _resident((SG_CHUNK, BRANCH_W)),
            _resident((BRANCH_W, BRANCH_W)),
            _resident((1, BRANCH_W)),
            _resident((CONV_HALO, BRANCH_W)),
            _resident((1, BRANCH_W)),
            _resident((1, BRANCH_W)),
            _resident((1, BRANCH_W)),
        ],
        out_specs=(
            pl.BlockSpec((None, ts, BRANCH_W), lambda bi, i: (bi, i, 0)),
            pl.BlockSpec((None, nkb, BRANCH_W, KEY_BLOCK), lambda bi, i: (bi, i, 0, 0)),
            pl.BlockSpec((None, ts, BRANCH_W), lambda bi, i: (bi, i, 0)),
            pl.BlockSpec((None, ts, 3 * BRANCH_W), lambda bi, i: (bi, i, 0)),
        ),
        out_shape=out_shape,
        scratch_shapes=[
            pltpu.VMEM((ts, IN_COLS), F32),
            pltpu.VMEM((ts + POOL_HALO, BRANCH_W), F32),
            pltpu.VMEM((ts + CONV_HALO, BRANCH_W), F32),
            pltpu.VMEM((SUBLANES, SG_CHUNK + POOL_HALO, LANES), F32),
            pltpu.VMEM((SUBLANES, SG_CHUNK + CONV_HALO, BRANCH_W), F32),
        ],
        compiler_params=pltpu.CompilerParams(
            dimension_semantics=("arbitrary", "arbitrary"), vmem_limit_bytes=VMEM_LIMIT_BYTES),
        name="mix_proj",
    )(x, w_in, sg_g, sg_b, sg_w, sg_bias, pool_w, pool_scale, conv_w, conv_b, cln_g, cln_b)


SB_TQ = 512


def _sb_attn_kernel(q_ref, kt_ref, v_ref, u_ref, o_ref,
                    q2_ref, dq_ref, z_ref, lk_ref, tot_ref, w_ref, carry_ref, acc_ref):
    i = pl.program_id(1)
    tq = SB_TQ
    n_pairs = BRANCH_W // LANES
    kb_max = i * (tq // KEY_BLOCK) + (tq // KEY_BLOCK - 1)
    low = lax.broadcasted_iota(jnp.int32, (KEY_BLOCK, LANES), 1) < HEAD_DIM
    low_q = lax.broadcasted_iota(jnp.int32, (tq, LANES), 1) < HEAD_DIM

    for p in range(n_pairs):
        qp = q_ref[:, p * LANES:(p + 1) * LANES]
        zero = jnp.zeros_like(qp)
        q2_ref[p] = jnp.concatenate([jnp.where(low_q, qp, zero), jnp.where(low_q, zero, qp)], axis=0)

    carry_ref[...] = jnp.zeros(carry_ref.shape, F32)
    acc_ref[...] = jnp.zeros(acc_ref.shape, F32)

    row = lax.broadcasted_iota(jnp.int32, (2 * tq, KEY_BLOCK), 0)
    col = lax.broadcasted_iota(jnp.int32, (2 * tq, KEY_BLOCK), 1)
    dq_ref[...] = i * tq + jnp.where(row >= tq, row - tq, row) - col

    def stacked_rows(ref_rows, a):
        return jnp.concatenate([ref_rows(a, tq), ref_rows(tq + a, 2 * tq)], axis=0)

    def iteration(t, s, mask1=False, mask2=False, skip=0, skip2=0, s1=True, s2=True, s3=True):
        a = skip * KEY_BLOCK
        nh = tq - a
        a2 = skip2 * KEY_BLOCK
        nh2 = tq - a2
        o = 1 - s
        kb1 = jnp.maximum(kb_max - t, 0)
        kb2 = kb_max - t + 1
        kb3 = jnp.clip(kb_max - t + 2, 0, kb_max)
        v_rows = pl.ds(pl.multiple_of(kb3 * KEY_BLOCK, KEY_BLOCK), KEY_BLOCK)

        zs, rs = [], []
        if s1:
            for p in range(n_pairs):
                lhs = q2_ref[p] if a == 0 else stacked_rows(lambda r0, r1, p=p: q2_ref[p, r0:r1, :], a)
                zs.append(_dot(lhs, kt_ref[kb1, p * LANES:(p + 1) * LANES, :]))
        if s2:
            for p in range(n_pairs):
                lhs = lk_ref[o, p] if a2 == 0 else stacked_rows(lambda r0, r1, p=p: lk_ref[o, p, r0:r1, :], a2)
                rs.append(_dot(lhs, u_ref[...]))
        if s3:
            v2s = []
            for p in range(n_pairs):
                vp = v_ref[v_rows, p * LANES:(p + 1) * LANES]
                vzero = jnp.zeros_like(vp)
                v2s.append(jnp.concatenate([jnp.where(low, vp, vzero), jnp.where(low, vzero, vp)], axis=0))
            vz = jnp.zeros_like(v2s[0])
            rhs = jnp.concatenate(
                [jnp.concatenate([v2s[0], vz], axis=1), jnp.concatenate([vz, v2s[1]], axis=1)], axis=0)
            pv = _dot(jnp.concatenate([w_ref[s, 0], w_ref[s, 1]], axis=1), rhs)

        if s1:
            for p in range(n_pairs):
                z = zs[p]
                nz = -z
                lk = jnp.minimum(nz, 0.0) - jnp.log(1.0 + jnp.exp2(jnp.minimum(z, nz))) * LOG2E
                if mask1:
                    dq = dq_ref[...] if a == 0 else stacked_rows(lambda r0, r1: dq_ref[r0:r1, :], a)
                    lk = jnp.where(kb1 * KEY_BLOCK < dq, lk, 0.0)
                lkb = lk.astype(BF16)
                tot = jnp.broadcast_to(jnp.sum(lk, axis=1, keepdims=True), lk.shape)
                for hh in range(2):
                    dst = slice(hh * tq + a, (hh + 1) * tq)
                    src = slice(hh * nh, (hh + 1) * nh)
                    z_ref[s, p, dst, :] = z[src]
                    lk_ref[s, p, dst, :] = lkb[src]
                    tot_ref[s, p, dst, :] = tot[src]
                    if a:
                        pad = slice(hh * tq, hh * tq + a)
                        z_ref[s, p, pad, :] = jnp.zeros((a, KEY_BLOCK), F32)
                        lk_ref[s, p, pad, :] = jnp.zeros((a, KEY_BLOCK), BF16)
                        tot_ref[s, p, pad, :] = jnp.zeros((a, KEY_BLOCK), F32)

        if s2:
            for p in range(n_pairs):
                if a2 == 0:
                    w = jnp.exp2(z_ref[o, p] + rs[p] + carry_ref[p])
                    if mask2:
                        w = jnp.where(kb2 * KEY_BLOCK < dq_ref[...], w, 0.0)
                    wb = w.astype(BF16)
                    w_ref[o, p] = jnp.concatenate([wb[0:tq], wb[tq:2 * tq]], axis=1)
                    carry_ref[p] = carry_ref[p] + tot_ref[o, p]
                else:
                    z_o = stacked_rows(lambda r0, r1, p=p: z_ref[o, p, r0:r1, :], a2)
                    c_o = stacked_rows(lambda r0, r1, p=p: carry_ref[p, r0:r1, :], a2)
                    w = jnp.exp2(z_o + rs[p] + c_o)
                    if mask2:
                        w = jnp.where(kb2 * KEY_BLOCK < stacked_rows(lambda r0, r1: dq_ref[r0:r1, :], a2), w, 0.0)
                    wb = w.astype(BF16)
                    w_ref[o, p, a2:tq, :] = jnp.concatenate([wb[0:nh2], wb[nh2:2 * nh2]], axis=1)
                    w_ref[o, p, 0:a2, :] = jnp.zeros((a2, 2 * KEY_BLOCK), BF16)
                    for hh in range(2):
                        dst = slice(hh * tq + a2, (hh + 1) * tq)
                        carry_ref[p, dst, :] = c_o[hh * nh2:(hh + 1) * nh2] + tot_ref[o, p, dst, :]

        if s3:
            acc_ref[...] = acc_ref[...] + pv

    n_diag = tq // KEY_BLOCK
    n_items = n_diag * (i + 1)
    head = n_diag + 2
    for t in range(head):
        iteration(t, t % 2, mask1=t < n_diag, mask2=t - 1 < n_diag, skip=max(n_diag - 1 - t, 0),
                  skip2=min(max(n_diag - t, 0), n_diag - 1), s2=t >= 1, s3=t >= 2)

    n_mid = jnp.maximum(n_items - head, 0)
    n_quads = n_mid // 4

    def middle(t0, count):
        for k in range(count):
            iteration(t0 + k, k % 2)

    def quad(tt, carry):
        middle(head + 4 * tt, 4)
        return carry

    lax.fori_loop(0, n_quads, quad, 0)

    @pl.when(n_mid % 4 == 2)
    def _():
        middle(head + 4 * n_quads, 2)

    @pl.when(i >= 1)
    def _():
        iteration(n_items, 0, s1=False)
        iteration(n_items + 1, 1, s1=False, s2=False)

    o_ref[...] = acc_ref[...].astype(BF16)


def _sb_attn(q, kt, v, u_tri):
    b, s, _ = q.shape
    tq = SB_TQ
    n_pairs = BRANCH_W // LANES
    return pl.pallas_call(
        _sb_attn_kernel,
        grid=(b, s // tq),
        in_specs=[
            pl.BlockSpec((None, tq, BRANCH_W), lambda bi, i: (bi, i, 0)),
            pl.BlockSpec((None, s // KEY_BLOCK, BRANCH_W, KEY_BLOCK), lambda bi, i: (bi, 0, 0, 0)),
            pl.BlockSpec((None, s, BRANCH_W), lambda bi, i: (bi, 0, 0)),
            _resident((KEY_BLOCK, KEY_BLOCK)),
        ],
        out_specs=pl.BlockSpec((None, tq, BRANCH_W), lambda bi, i: (bi, i, 0)),
        out_shape=jax.ShapeDtypeStruct((b, s, BRANCH_W), BF16),
        scratch_shapes=[
            pltpu.VMEM((n_pairs, 2 * tq, LANES), BF16),
            pltpu.VMEM((2 * tq, KEY_BLOCK), jnp.int32),
            pltpu.VMEM((2, n_pairs, 2 * tq, KEY_BLOCK), F32),
            pltpu.VMEM((2, n_pairs, 2 * tq, KEY_BLOCK), BF16),
            pltpu.VMEM((2, n_pairs, 2 * tq, KEY_BLOCK), F32),
            pltpu.VMEM((2, n_pairs, tq, 2 * KEY_BLOCK), BF16),
            pltpu.VMEM((n_pairs, 2 * tq, KEY_BLOCK), F32),
            pltpu.VMEM((tq, BRANCH_W), F32),
        ],
        compiler_params=pltpu.CompilerParams(
            dimension_semantics=("arbitrary", "arbitrary"), vmem_limit_bytes=VMEM_LIMIT_BYTES),
        name="sb_attn",
    )(q, kt, v, u_tri)


def _mix_merge_kernel(x_ref, ya_ref, y_ref, gw_ref, gb_ref, bw_ref, ow_ref, g_ref, b_ref, o_ref, m_ref, *, tn, sub):
    for r0 in range(0, x_ref.shape[0], sub):
        rows = slice(r0, r0 + sub)
        x = x_ref[rows, :]
        xb = x.astype(BF16)
        ys = [ya_ref[rows, :]] + [y_ref[rows, k * BRANCH_W:(k + 1) * BRANCH_W] for k in range(3)]
        for c0 in range(0, D_MODEL, tn):
            cols = slice(c0, c0 + tn)
            merged = None
            for n in range(4):
                gate = jax.nn.sigmoid(_dot(xb, gw_ref[n, :, cols]) + gb_ref[n:n + 1, cols])
                term = gate * _dot(ys[n], bw_ref[n, :, cols])
                merged = term if merged is None else merged + term
            m_ref[rows, cols] = merged.astype(BF16)
        y = DN_ALPHA * x + _dot(m_ref[rows, :], ow_ref[...])
        o_ref[rows, :] = _layer_norm(y, g_ref[...], b_ref[...])


def _mix_merge(x2, ya, ybcd, gate_w, gate_b, branch_w, out_w, ln_g, ln_b, *, tm=1024, tn=256, sub=512):
    n = x2.shape[0]
    return pl.pallas_call(
        functools.partial(_mix_merge_kernel, tn=tn, sub=sub),
        grid=(n // tm,),
        in_specs=[
            pl.BlockSpec((tm, D_MODEL), lambda i: (i, 0)),
            pl.BlockSpec((tm, BRANCH_W), lambda i: (i, 0)),
            pl.BlockSpec((tm, 3 * BRANCH_W), lambda i: (i, 0)),
            _resident((4, D_MODEL, D_MODEL)),
            _resident((4, D_MODEL)),
            _resident((4, BRANCH_W, D_MODEL)),
            _resident((D_MODEL, D_MODEL)),
            _resident((1, D_MODEL)),
            _resident((1, D_MODEL)),
        ],
        out_specs=pl.BlockSpec((tm, D_MODEL), lambda i: (i, 0)),
        out_shape=jax.ShapeDtypeStruct((n, D_MODEL), F32),
        scratch_shapes=[pltpu.VMEM((tm, D_MODEL), BF16)],
        compiler_params=pltpu.CompilerParams(
            dimension_semantics=("arbitrary",), vmem_limit_bytes=VMEM_LIMIT_BYTES),
        name="mix_merge",
    )(x2, ya, ybcd, gate_w, gate_b, branch_w, out_w, ln_g, ln_b)


def _suffix_sum_matrix():
    j = jnp.arange(KEY_BLOCK)[:, None]
    s = jnp.arange(KEY_BLOCK)[None, :]
    return (j >= s).astype(BF16)


def _block_diag(w):
    g, r, c = w.shape
    out = jnp.zeros((g * r, g * c), w.dtype)
    for k in range(g):
        out = out.at[k * r:(k + 1) * r, k * c:(k + 1) * c].set(w[k])
    return out


def kernel(x, ln_g, ln_b, ffn_w_in, ffn_w_out, mix_w_in, gate_w, gate_b, branch_w, out_w, sg_ln_g, sg_ln_b, sg_w,
           sg_b, pool_w, pool_scale, conv_w, conv_b, conv_ln_g, conv_ln_b):
    b, s, d = x.shape
    n = b * s
    u_ext = _suffix_sum_matrix()
    row = lambda a: a.reshape(1, -1)
    x2 = x.reshape(n, d)
    for l in range(DEPTH):
        x2 = _ffn(x2, row(ln_g[l, 0]), row(ln_b[l, 0]), ffn_w_in[l, 0].astype(BF16), ffn_w_out[l, 0].astype(BF16))
        sg_bias = jnp.repeat(sg_b[l].T, BRANCH_W // SG_GROUPS, axis=1)
        conv_w_pad = jnp.pad(conv_w[l], ((0, CONV_HALO - CONV_W), (0, 0)))
        q, kt, v, ybcd = _mix_proj(
            x2.reshape(b, s, d), mix_w_in[l].astype(BF16), row(sg_ln_g[l]), row(sg_ln_b[l]), sg_w[l], sg_bias,
            _block_diag(pool_w[l]).astype(BF16), row(pool_scale[l]), conv_w_pad, row(conv_b[l]),
            row(conv_ln_g[l]), row(conv_ln_b[l]))
        ya = _sb_attn(q, kt, v, u_ext)
        x2 = _mix_merge(x2, ya.reshape(n, BRANCH_W), ybcd.reshape(n, 3 * BRANCH_W), gate_w[l].astype(BF16),
                        gate_b[l], branch_w[l].astype(BF16), out_w[l].astype(BF16), row(ln_g[l, 1]), row(ln_b[l, 1]))
        x2 = _ffn(x2, row(ln_g[l, 2]), row(ln_b[l, 2]), ffn_w_in[l, 1].astype(BF16), ffn_w_out[l, 1].astype(BF16))
    return x2.reshape(b, s, d)
```

```python
import functools
import math

import jax
import jax.numpy as jnp
from jax import lax
from jax.experimental import pallas as pl
from jax.experimental.pallas import tpu as pltpu

D_MODEL = 1024
DEPTH = 2
BRANCH_W = D_MODEL // 4
HEAD_DIM = 64
SG_CHUNK = 128
SG_GROUPS = 4
POOL_WINDOWS = (2, 4, 8, 16)
CONV_W = 31
D_FF = 2816
IN_COLS = 8 * BRANCH_W
LN_EPS = 1e-5
DN_ALPHA = (2.0 * DEPTH) ** 0.25
LOG2E = 1.0 / math.log(2.0)

LANES = 128
SUBLANES = 8
VMEM_LIMIT_BYTES = 56 * 1024 * 1024

KEY_BLOCK = 128
POOL_HALO = 16
CONV_HALO = 32
PROJ_COLS = 512

F32 = jnp.float32
BF16 = jnp.bfloat16


def _dot(a, b):
    return jnp.dot(a, b, preferred_element_type=F32)


def _layer_norm(y, g, b):
    mu = jnp.mean(y, axis=-1, keepdims=True)
    d = y - mu
    var = jnp.mean(d * d, axis=-1, keepdims=True)
    return d * lax.rsqrt(var + LN_EPS) * g + b


def _resident(shape):
    zeros = (0,) * len(shape)
    return pl.BlockSpec(shape, lambda *_: zeros, pipeline_mode=pl.Buffered(1))


def _ffn_kernel(x_ref, g_ref, b_ref, win_ref, wout_ref, o_ref, *, tf, sub):
    for r0 in range(0, x_ref.shape[0], sub):
        rows = slice(r0, r0 + sub)
        x = x_ref[rows, :]
        xb = x.astype(BF16)
        acc = None
        for c in range(D_FF // tf):
            gate = _dot(xb, win_ref[:, c * tf:(c + 1) * tf])
            up = _dot(xb, win_ref[:, D_FF + c * tf:D_FF + (c + 1) * tf])
            h = (gate * jax.nn.sigmoid(gate) * up).astype(BF16)
            part = _dot(h, wout_ref[c * tf:(c + 1) * tf, :])
            acc = part if acc is None else acc + part
        y = DN_ALPHA * x + 0.5 * acc
        o_ref[rows, :] = _layer_norm(y, g_ref[...], b_ref[...])


def _ffn(x2, ln_g, ln_b, w_in, w_out, *, tm=1024, tf=256, sub=512):
    n = x2.shape[0]
    return pl.pallas_call(
        functools.partial(_ffn_kernel, tf=tf, sub=sub),
        grid=(n // tm,),
        in_specs=[
            pl.BlockSpec((tm, D_MODEL), lambda i: (i, 0)),
            _resident((1, D_MODEL)),
            _resident((1, D_MODEL)),
            _resident((D_MODEL, 2 * D_FF)),
            _resident((D_FF, D_MODEL)),
        ],
        out_specs=pl.BlockSpec((tm, D_MODEL), lambda i: (i, 0)),
        out_shape=jax.ShapeDtypeStruct((n, D_MODEL), F32),
        compiler_params=pltpu.CompilerParams(
            dimension_semantics=("arbitrary",), vmem_limit_bytes=VMEM_LIMIT_BYTES),
        name="ffn",
    )(x2, ln_g, ln_b, w_in, w_out)


def _gelu_tanh(x):
    c = math.sqrt(2.0 / math.pi)
    return x * (0.5 * (1.0 + jnp.tanh(c * (x + 0.044715 * (x * x * x)))))


def _mix_proj_kernel(x_ref, win_ref, sg_g_ref, sg_b_ref, sgw_ref, sgbias_ref, poolw_ref, pscale_ref,
                     convw_ref, convb_ref, cln_g_ref, cln_b_ref,
                     q_ref, kt_ref, v_ref, y_ref,
                     h_ref, pbuf, ybuf, pshift, cshift, *, ts):
    i = pl.program_id(1)
    rc = SG_CHUNK
    n_chunks = ts // rc

    @pl.when(i == 0)
    def _():
        pbuf[0:POOL_HALO, :] = jnp.zeros((POOL_HALO, BRANCH_W), F32)
        ybuf[0:CONV_HALO, :] = jnp.zeros((CONV_HALO, BRANCH_W), F32)

    for r0 in range(0, ts, ts // 2):
        xb = x_ref[r0:r0 + ts // 2, :].astype(BF16)
        for c0 in range(0, IN_COLS, PROJ_COLS):
            h_ref[r0:r0 + ts // 2, c0:c0 + PROJ_COLS] = _dot(xb, win_ref[:, c0:c0 + PROJ_COLS])

    tri = (lax.broadcasted_iota(jnp.int32, (SG_CHUNK, SG_CHUNK), 0)
           >= lax.broadcasted_iota(jnp.int32, (SG_CHUNK, SG_CHUNK), 1))
    w_groups = [jnp.where(tri, sgw_ref[g], 0.0).astype(BF16) for g in range(SG_GROUPS)]

    lane256 = lax.broadcasted_iota(jnp.int32, (rc, BRANCH_W), 1)
    lane_group = lane256 // (BRANCH_W // SG_GROUPS)
    lane128 = lax.broadcasted_iota(jnp.int32, (rc, LANES), 1)
    low_half = lane128 < (LANES // 2)
    row_f = lax.broadcasted_iota(jnp.int32, (rc, LANES), 0)

    for c in range(n_chunks):
        r0 = c * rc
        rows = slice(r0, r0 + rc)

        q_ref[rows, :] = (h_ref[rows, 0:BRANCH_W] * (HEAD_DIM ** -0.5 * LOG2E)).astype(BF16)
        kt_ref[c] = h_ref[rows, BRANCH_W:2 * BRANCH_W].T.astype(BF16)
        v_ref[rows, :] = h_ref[rows, 2 * BRANCH_W:3 * BRANCH_W].astype(BF16)

        u = _gelu_tanh(h_ref[rows, 3 * BRANCH_W:4 * BRANCH_W])
        vv = _gelu_tanh(h_ref[rows, 4 * BRANCH_W:5 * BRANCH_W])
        vn = _layer_norm(vv, sg_g_ref[...], sg_b_ref[...]).astype(BF16)
        mixed = None
        for g in range(SG_GROUPS):
            r = _dot(w_groups[g], vn)
            mixed = r if mixed is None else jnp.where(lane_group == g, r, mixed)
        y_ref[rows, 0:BRANCH_W] = (u * (mixed + sgbias_ref[...])).astype(BF16)

        p = h_ref[rows, 5 * BRANCH_W:6 * BRANCH_W]
        pbuf[POOL_HALO + r0:POOL_HALO + r0 + rc, :] = p
        pos1 = (i * ts + r0 + 1 + row_f).astype(F32)
        pooled = []
        for half, (w_lo, w_hi) in enumerate(((POOL_WINDOWS[0], POOL_WINDOWS[1]),
                                             (POOL_WINDOWS[2], POOL_WINDOWS[3]))):
            cols = slice(half * LANES, (half + 1) * LANES)
            s_lo = s_hi = None
            for k in range(SUBLANES):
                starts = [POOL_HALO + r0 - d for d in range(w_hi) if (POOL_HALO + r0 - d) % SUBLANES == k]
                if not starts:
                    continue
                n_rows = max(starts) - min(starts) + rc
                pshift[k, 0:n_rows, :] = pbuf[pl.ds(min(starts), n_rows), cols]
                for st in starts:
                    term = pshift[k, st - min(starts):st - min(starts) + rc, :]
                    if POOL_HALO + r0 - st < w_lo:
                        s_lo = term if s_lo is None else s_lo + term
                    else:
                        s_hi = term if s_hi is None else s_hi + term
            s_hi = s_hi + s_lo
            cnt = jnp.minimum(pos1, jnp.where(low_half, float(w_lo), float(w_hi)))
            pooled.append(jnp.where(low_half, s_lo, s_hi) / cnt - p[:, cols])
        pooled = jnp.concatenate(pooled, axis=1).astype(BF16)
        y_ref[rows, BRANCH_W:2 * BRANCH_W] = (_dot(pooled, poolw_ref[...]) * pscale_ref[...]).astype(BF16)

        a = h_ref[rows, 6 * BRANCH_W:7 * BRANCH_W]
        gl = h_ref[rows, 7 * BRANCH_W:8 * BRANCH_W]
        ybuf[CONV_HALO + r0:CONV_HALO + r0 + rc, :] = a * jax.nn.sigmoid(gl)
        conv = None
        base = CONV_HALO + r0 - (CONV_W - 1)
        for k in range(SUBLANES):
            starts = [base + j for j in range(CONV_W) if (base + j) % SUBLANES == k]
            n_rows = max(starts) - min(starts) + rc
            cshift[k, 0:n_rows, :] = ybuf[pl.ds(min(starts), n_rows), :]
            for st in starts:
                tap = cshift[k, st - min(starts):st - min(starts) + rc, :] * convw_ref[st - base:st - base + 1, :]
                conv = tap if conv is None else conv + tap
        cn = _layer_norm(conv + convb_ref[...], cln_g_ref[...], cln_b_ref[...])
        y_ref[rows, 2 * BRANCH_W:3 * BRANCH_W] = (cn * jax.nn.sigmoid(cn)).astype(BF16)

    pbuf[0:POOL_HALO, :] = pbuf[ts:ts + POOL_HALO, :]
    ybuf[0:CONV_HALO, :] = ybuf[ts:ts + CONV_HALO, :]


def _mix_proj(x, w_in, sg_g, sg_b, sg_w, sg_bias, pool_w, pool_scale, conv_w, conv_b, cln_g, cln_b, *, ts=512):
    b, s, _ = x.shape
    nkb = ts // KEY_BLOCK
    out_shape = (
        jax.ShapeDtypeStruct((b, s, BRANCH_W), BF16),
        jax.ShapeDtypeStruct((b, s // KEY_BLOCK, BRANCH_W, KEY_BLOCK), BF16),
        jax.ShapeDtypeStruct((b, s, BRANCH_W), BF16),
        jax.ShapeDtypeStruct((b, s, 3 * BRANCH_W), BF16),
    )
    return pl.pallas_call(
        functools.partial(_mix_proj_kernel, ts=ts),
        grid=(b, s // ts),
        in_specs=[
            pl.BlockSpec((None, ts, D_MODEL), lambda bi, i: (bi, i, 0)),
            _resident((D_MODEL, IN_COLS)),
            _resident((1, BRANCH_W)),
            _resident((1, BRANCH_W)),
            _resident((SG_GROUPS, SG_CHUNK, SG_CHUNK)),
            _resident((SG_CHUNK, BRANCH_W)),
            _resident((BRANCH_W, BRANCH_W)),
            _resident((1, BRANCH_W)),
            _resident((CONV_HALO, BRANCH_W)),
            _resident((1, BRANCH_W)),
            _resident((1, BRANCH_W)),
            _resident((1, BRANCH_W)),
        ],
        out_specs=(
            pl.BlockSpec((None, ts, BRANCH_W), lambda bi, i: (bi, i, 0)),
            pl.BlockSpec((None, nkb, BRANCH_W, KEY_BLOCK), lambda bi, i: (bi, i, 0, 0)),
            pl.BlockSpec((None, ts, BRANCH_W), lambda bi, i: (bi, i, 0)),
            pl.BlockSpec((None, ts, 3 * BRANCH_W), lambda bi, i: (bi, i, 0)),
        ),
        out_shape=out_shape,
        scratch_shapes=[
            pltpu.VMEM((ts, IN_COLS), F32),
            pltpu.VMEM((ts + POOL_HALO, BRANCH_W), F32),
            pltpu.VMEM((ts + CONV_HALO, BRANCH_W), F32),
            pltpu.VMEM((SUBLANES, SG_CHUNK + POOL_HALO, LANES), F32),
            pltpu.VMEM((SUBLANES, SG_CHUNK + CONV_HALO, BRANCH_W), F32),
        ],
        compiler_params=pltpu.CompilerParams(
            dimension_semantics=("arbitrary", "arbitrary"), vmem_limit_bytes=VMEM_LIMIT_BYTES),
        name="mix_proj",
    )(x, w_in, sg_g, sg_b, sg_w, sg_bias, pool_w, pool_scale, conv_w, conv_b, cln_g, cln_b)


SB_TQ = 512


def _sb_attn_kernel(q_ref, kt_ref, v_ref, u_ref, o_ref,
                    q2_ref, dq_ref, z_ref, lk_ref, tot_ref, w_ref, carry_ref, acc_ref):
    i = pl.program_id(1)
    tq = SB_TQ
    n_pairs = BRANCH_W // LANES
    kb_max = i * (tq // KEY_BLOCK) + (tq // KEY_BLOCK - 1)
    low = lax.broadcasted_iota(jnp.int32, (KEY_BLOCK, LANES), 1) < HEAD_DIM
    low_q = lax.broadcasted_iota(jnp.int32, (tq, LANES), 1) < HEAD_DIM

    for p in range(n_pairs):
        qp = q_ref[:, p * LANES:(p + 1) * LANES]
        zero = jnp.zeros_like(qp)
        q2_ref[p] = jnp.concatenate([jnp.where(low_q, qp, zero), jnp.where(low_q, zero, qp)], axis=0)

    carry_ref[...] = jnp.zeros(carry_ref.shape, F32)
    acc_ref[...] = jnp.zeros(acc_ref.shape, F32)

    row = lax.broadcasted_iota(jnp.int32, (2 * tq, KEY_BLOCK), 0)
    col = lax.broadcasted_iota(jnp.int32, (2 * tq, KEY_BLOCK), 1)
    dq_ref[...] = i * tq + jnp.where(row >= tq, row - tq, row) - col

    def stacked_rows(ref_rows, a):
        return jnp.concatenate([ref_rows(a, tq), ref_rows(tq + a, 2 * tq)], axis=0)

    def iteration(t, s, mask1=False, mask2=False, skip=0, skip2=0, s1=True, s2=True, s3=True):
        a = skip * KEY_BLOCK
        nh = tq - a
        a2 = skip2 * KEY_BLOCK
        nh2 = tq - a2
        o = 1 - s
        kb1 = jnp.maximum(kb_max - t, 0)
        kb2 = kb_max - t + 1
        kb3 = jnp.clip(kb_max - t + 2, 0, kb_max)
        v_rows = pl.ds(pl.multiple_of(kb3 * KEY_BLOCK, KEY_BLOCK), KEY_BLOCK)

        zs, rs = [], []
        if s1:
            for p in range(n_pairs):
                lhs = q2_ref[p] if a == 0 else stacked_rows(lambda r0, r1, p=p: q2_ref[p, r0:r1, :], a)
                zs.append(_dot(lhs, kt_ref[kb1, p * LANES:(p + 1) * LANES, :]))
        if s2:
            for p in range(n_pairs):
                lhs = lk_ref[o, p] if a2 == 0 else stacked_rows(lambda r0, r1, p=p: lk_ref[o, p, r0:r1, :], a2)
                rs.append(_dot(lhs, u_ref[...]))
        if s3:
            v2s = []
            for p in range(n_pairs):
                vp = v_ref[v_rows, p * LANES:(p + 1) * LANES]
                vzero = jnp.zeros_like(vp)
                v2s.append(jnp.concatenate([jnp.where(low, vp, vzero), jnp.where(low, vzero, vp)], axis=0))
            vz = jnp.zeros_like(v2s[0])
            rhs = jnp.concatenate(
                [jnp.concatenate([v2s[0], vz], axis=1), jnp.concatenate([vz, v2s[1]], axis=1)], axis=0)
            pv = _dot(jnp.concatenate([w_ref[s, 0], w_ref[s, 1]], axis=1), rhs)

        if s1:
            for p in range(n_pairs):
                z = zs[p]
                nz = -z
                lk = jnp.minimum(nz, 0.0) - jnp.log(1.0 + jnp.exp2(jnp.minimum(z, nz))) * LOG2E
                if mask1:
                    dq = dq_ref[...] if a == 0 else stacked_rows(lambda r0, r1: dq_ref[r0:r1, :], a)
                    lk = jnp.where(kb1 * KEY_BLOCK < dq, lk, 0.0)
                lkb = lk.astype(BF16)
                tot = jnp.broadcast_to(jnp.sum(lk, axis=1, keepdims=True), lk.shape)
                for hh in range(2):
                    dst = slice(hh * tq + a, (hh + 1) * tq)
                    src = slice(hh * nh, (hh + 1) * nh)
                    z_ref[s, p, dst, :] = z[src]
                    lk_ref[s, p, dst, :] = lkb[src]
                    tot_ref[s, p, dst, :] = tot[src]
                    if a:
                        pad = slice(hh * tq, hh * tq + a)
                        z_ref[s, p, pad, :] = jnp.zeros((a, KEY_BLOCK), F32)
                        lk_ref[s, p, pad, :] = jnp.zeros((a, KEY_BLOCK), BF16)
                        tot_ref[s, p, pad, :] = jnp.zeros((a, KEY_BLOCK), F32)

        if s2:
            for p in range(n_pairs):
                if a2 == 0:
                    w = jnp.exp2(z_ref[o, p] + rs[p] + carry_ref[p])
                    if mask2:
                        w = jnp.where(kb2 * KEY_BLOCK < dq_ref[...], w, 0.0)
                    wb = w.astype(BF16)
                    w_ref[o, p] = jnp.concatenate([wb[0:tq], wb[tq:2 * tq]], axis=1)
                    carry_ref[p] = carry_ref[p] + tot_ref[o, p]
                else:
                    z_o = stacked_rows(lambda r0, r1, p=p: z_ref[o, p, r0:r1, :], a2)
                    c_o = stacked_rows(lambda r0, r1, p=p: carry_ref[p, r0:r1, :], a2)
                    w = jnp.exp2(z_o + rs[p] + c_o)
                    if mask2:
                        w = jnp.where(kb2 * KEY_BLOCK < stacked_rows(lambda r0, r1: dq_ref[r0:r1, :], a2), w, 0.0)
                    wb = w.astype(BF16)
                    w_ref[o, p, a2:tq, :] = jnp.concatenate([wb[0:nh2], wb[nh2:2 * nh2]], axis=1)
                    w_ref[o, p, 0:a2, :] = jnp.zeros((a2, 2 * KEY_BLOCK), BF16)
                    for hh in range(2):
                        dst = slice(hh * tq + a2, (hh + 1) * tq)
                        carry_ref[p, dst, :] = c_o[hh * nh2:(hh + 1) * nh2] + tot_ref[o, p, dst, :]

        if s3:
            acc_ref[...] = acc_ref[...] + pv

    n_diag = tq // KEY_BLOCK
    n_items = n_diag * (i + 1)
    head = n_diag + 2
    for t in range(head):
        iteration(t, t % 2, mask1=t < n_diag, mask2=t - 1 < n_diag, skip=max(n_diag - 1 - t, 0),
                  skip2=min(max(n_diag - t, 0), n_diag - 1), s2=t >= 1, s3=t >= 2)

    n_mid = jnp.maximum(n_items - head, 0)
    n_octs = n_mid // 8

    def middle(t0, count):
        for k in range(count):
            iteration(t0 + k, k % 2)

    def octet(tt, carry):
        middle(head + 8 * tt, 8)
        return carry

    lax.fori_loop(0, n_octs, octet, 0)
    has_quad = (n_mid % 8) // 4

    @pl.when(has_quad == 1)
    def _():
        middle(head + 8 * n_octs, 4)

    @pl.when(n_mid % 4 == 2)
    def _():
        middle(head + 8 * n_octs + 4 * has_quad, 2)

    @pl.when(i >= 1)
    def _():
        iteration(n_items, 0, s1=False)
        iteration(n_items + 1, 1, s1=False, s2=False)

    o_ref[...] = acc_ref[...].astype(BF16)


def _sb_attn(q, kt, v, u_tri):
    b, s, _ = q.shape
    tq = SB_TQ
    n_pairs = BRANCH_W // LANES
    return pl.pallas_call(
        _sb_attn_kernel,
        grid=(b, s // tq),
        in_specs=[
            pl.BlockSpec((None, tq, BRANCH_W), lambda bi, i: (bi, i, 0)),
            pl.BlockSpec((None, s // KEY_BLOCK, BRANCH_W, KEY_BLOCK), lambda bi, i: (bi, 0, 0, 0)),
            pl.BlockSpec((None, s, BRANCH_W), lambda bi, i: (bi, 0, 0)),
            _resident((KEY_BLOCK, KEY_BLOCK)),
        ],
        out_specs=pl.BlockSpec((None, tq, BRANCH_W), lambda bi, i: (bi, i, 0)),
        out_shape=jax.ShapeDtypeStruct((b, s, BRANCH_W), BF16),
        scratch_shapes=[
            pltpu.VMEM((n_pairs, 2 * tq, LANES), BF16),
            pltpu.VMEM((2 * tq, KEY_BLOCK), jnp.int32),
            pltpu.VMEM((2, n_pairs, 2 * tq, KEY_BLOCK), F32),
            pltpu.VMEM((2, n_pairs, 2 * tq, KEY_BLOCK), BF16),
            pltpu.VMEM((2, n_pairs, 2 * tq, KEY_BLOCK), F32),
            pltpu.VMEM((2, n_pairs, tq, 2 * KEY_BLOCK), BF16),
            pltpu.VMEM((n_pairs, 2 * tq, KEY_BLOCK), F32),
            pltpu.VMEM((tq, BRANCH_W), F32),
        ],
        compiler_params=pltpu.CompilerParams(
            dimension_semantics=("arbitrary", "arbitrary"), vmem_limit_bytes=VMEM_LIMIT_BYTES),
        name="sb_attn",
    )(q, kt, v, u_tri)


def _mix_merge_kernel(x_ref, ya_ref, y_ref, gw_ref, gb_ref, bw_ref, ow_ref, g_ref, b_ref, o_ref, m_ref, *, tn, sub):
    for r0 in range(0, x_ref.shape[0], sub):
        rows = slice(r0, r0 + sub)
        x = x_ref[rows, :]
        xb = x.astype(BF16)
        ys = [ya_ref[rows, :]] + [y_ref[rows, k * BRANCH_W:(k + 1) * BRANCH_W] for k in range(3)]
        for c0 in range(0, D_MODEL, tn):
            cols = slice(c0, c0 + tn)
            merged = None
            for n in range(4):
                gate = jax.nn.sigmoid(_dot(xb, gw_ref[n, :, cols]) + gb_ref[n:n + 1, cols])
                term = gate * _dot(ys[n], bw_ref[n, :, cols])
                merged = term if merged is None else merged + term
            m_ref[rows, cols] = merged.astype(BF16)
        y = DN_ALPHA * x + _dot(m_ref[rows, :], ow_ref[...])
        o_ref[rows, :] = _layer_norm(y, g_ref[...], b_ref[...])


def _mix_merge(x2, ya, ybcd, gate_w, gate_b, branch_w, out_w, ln_g, ln_b, *, tm=1024, tn=256, sub=512):
    n = x2.shape[0]
    return pl.pallas_call(
        functools.partial(_mix_merge_kernel, tn=tn, sub=sub),
        grid=(n // tm,),
        in_specs=[
            pl.BlockSpec((tm, D_MODEL), lambda i: (i, 0)),
            pl.BlockSpec((tm, BRANCH_W), lambda i: (i, 0)),
            pl.BlockSpec((tm, 3 * BRANCH_W), lambda i: (i, 0)),
            _resident((4, D_MODEL, D_MODEL)),
            _resident((4, D_MODEL)),
            _resident((4, BRANCH_W, D_MODEL)),
            _resident((D_MODEL, D_MODEL)),
            _resident((1, D_MODEL)),
            _resident((1, D_MODEL)),
        ],
        out_specs=pl.BlockSpec((tm, D_MODEL), lambda i: (i, 0)),
        out_shape=jax.ShapeDtypeStruct((n, D_MODEL), F32),
        scratch_shapes=[pltpu.VMEM((tm, D_MODEL), BF16)],
        compiler_params=pltpu.CompilerParams(
            dimension_semantics=("arbitrary",), vmem_limit_bytes=VMEM_LIMIT_BYTES),
        name="mix_merge",
    )(x2, ya, ybcd, gate_w, gate_b, branch_w, out_w, ln_g, ln_b)


def _suffix_sum_matrix():
    j = jnp.arange(KEY_BLOCK)[:, None]
    s = jnp.arange(KEY_BLOCK)[None, :]
    return (j >= s).astype(BF16)


def _block_diag(w):
    g, r, c = w.shape
    out = jnp.zeros((g * r, g * c), w.dtype)
    for k in range(g):
        out = out.at[k * r:(k + 1) * r, k * c:(k + 1) * c].set(w[k])
    return out


def kernel(x, ln_g, ln_b, ffn_w_in, ffn_w_out, mix_w_in, gate_w, gate_b, branch_w, out_w, sg_ln_g, sg_ln_b, sg_w,
           sg_b, pool_w, pool_scale, conv_w, conv_b, conv_ln_g, conv_ln_b):
    b, s, d = x.shape
    n = b * s
    u_ext = _suffix_sum_matrix()
    row = lambda a: a.reshape(1, -1)
    x2 = x.reshape(n, d)
    for l in range(DEPTH):
        x2 = _ffn(x2, row(ln_g[l, 0]), row(ln_b[l, 0]), ffn_w_in[l, 0].astype(BF16), ffn_w_out[l, 0].astype(BF16))
        sg_bias = jnp.repeat(sg_b[l].T, BRANCH_W // SG_GROUPS, axis=1)
        conv_w_pad = jnp.pad(conv_w[l], ((0, CONV_HALO - CONV_W), (0, 0)))
        q, kt, v, ybcd = _mix_proj(
            x2.reshape(b, s, d), mix_w_in[l].astype(BF16), row(sg_ln_g[l]), row(sg_ln_b[l]), sg_w[l], sg_bias,
            _block_diag(pool_w[l]).astype(BF16), row(pool_scale[l]), conv_w_pad, row(conv_b[l]),
            row(conv_ln_g[l]), row(conv_ln_b[l]))
        ya = _sb_attn(q, kt, v, u_ext)
        x2 = _mix_merge(x2, ya.reshape(n, BRANCH_W), ybcd.reshape(n, 3 * BRANCH_W), gate_w[l].astype(BF16),
                        gate_b[l], branch_w[l].astype(BF16), out_w[l].astype(BF16), row(ln_g[l, 1]), row(ln_b[l, 1]))
        x2 = _ffn(x2, row(ln_g[l, 2]), row(ln_b[l, 2]), ffn_w_in[l, 1].astype(BF16), ffn_w_out[l, 1].astype(BF16))
    return x2.reshape(b, s, d)
```
